```python
import math
import jax, jax.numpy as jnp
from jax import lax
import numpy as np

D_MODEL = 2048
BATCH = 4
SEQ = 8192
DEPTH = 1

SSM_EXPAND = 2
SSM_D_INNER = SSM_EXPAND * D_MODEL
SSM_HEAD_DIM = 64
SSM_HEADS = SSM_D_INNER // SSM_HEAD_DIM
SSM_GROUPS = 8
SSM_STATE = 128
SSM_CONV = 4
SSM_CHUNK = 256
SSM_CONV_DIM = SSM_D_INNER + 2 * SSM_GROUPS * SSM_STATE

ATTN_HEADS = 16
ATTN_HEAD_DIM = 128
ATTN_D = ATTN_HEADS * ATTN_HEAD_DIM
MOBA_BLOCK = 256
MOBA_TOPK = 3
Q_CHUNK = 16

N_BRANCH = 2
COL_SIZES = (SSM_D_INNER, SSM_CONV_DIM, SSM_HEADS, ATTN_D, ATTN_D, ATTN_D, ATTN_D, N_BRANCH * D_MODEL)
IN_COLS = sum(COL_SIZES)
EPS = 1e-6
NEG = -1e30

kernel_name = "hybrid_ssd_moba_gated_block"


def rmsnorm(x, w):
    xf = x.astype(jnp.float32)
    y = xf * lax.rsqrt(jnp.mean(xf * xf, axis=-1, keepdims=True) + EPS)
    return (y * w.astype(jnp.float32)).astype(x.dtype)


def group_rmsnorm(x, w, groups):
    shp = x.shape
    xf = x.astype(jnp.float32).reshape(shp[:-1] + (groups, shp[-1] // groups))
    y = xf * lax.rsqrt(jnp.mean(xf * xf, axis=-1, keepdims=True) + EPS)
    return (y.reshape(shp) * w.astype(jnp.float32)).astype(x.dtype)


def causal_dwconv(u, w, bias):
    k = w.shape[0]
    out = lax.conv_general_dilated(
        u, w[:, None, :].astype(u.dtype), window_strides=(1,), padding=[(k - 1, 0)],
        dimension_numbers=('NWC', 'WIO', 'NWC'), feature_group_count=u.shape[-1])
    return out + bias.astype(u.dtype)


def ssd_scan(xh, dt, A, Bm, Cm):
    b, s, h, p = xh.shape
    g, n = Bm.shape[2], Bm.shape[3]
    e = h // g
    L = SSM_CHUNK
    pad = (-s) % L
    if pad:
        xh = jnp.pad(xh, ((0, 0), (0, pad), (0, 0), (0, 0)))
        dt = jnp.pad(dt, ((0, 0), (0, pad), (0, 0)))
        Bm = jnp.pad(Bm, ((0, 0), (0, pad), (0, 0), (0, 0)))
        Cm = jnp.pad(Cm, ((0, 0), (0, pad), (0, 0), (0, 0)))
    nc = (s + pad) // L
    xc = xh.reshape(b, nc, L, g, e, p).transpose(1, 0, 2, 3, 4, 5)
    dtc = dt.reshape(b, nc, L, g, e).transpose(1, 0, 2, 3, 4)
    Bc = Bm.reshape(b, nc, L, g, n).transpose(1, 0, 2, 3, 4)
    Cc = Cm.reshape(b, nc, L, g, n).transpose(1, 0, 2, 3, 4)
    Ah = A.reshape(g, e)
    causal = jnp.tril(jnp.ones((L, L), dtype=bool))

    def step(state, inp):
        x_, dt_, B_, C_ = inp
        cum = jnp.cumsum(dt_ * Ah, axis=1)
        cum_t = cum.transpose(0, 2, 3, 1)
        seg = cum_t[..., :, None] - cum_t[..., None, :]
        decay = jnp.exp(jnp.where(causal, seg, -jnp.inf))
        cb = jnp.einsum('blgn,bsgn->bgls', C_, B_)
        w = cb[:, :, None] * decay * dt_.transpose(0, 2, 3, 1)[..., None, :]
        y_diag = jnp.einsum('bgels,bsgep->blgep', w, x_)
        y_off = jnp.einsum('blgn,bgepn->blgep', C_, state) * jnp.exp(cum)[..., None]
        decay_end = jnp.exp(cum[:, -1:] - cum) * dt_
        new_state = state * jnp.exp(cum[:, -1])[..., None, None] + \
            jnp.einsum('blge,blgep,blgn->bgepn', decay_end, x_, B_)
        return new_state, y_diag + y_off

    state0 = jnp.zeros((b, g, e, p, n), jnp.float32)
    _, ys = lax.scan(step, state0, (xc, dtc, Bc, Cc))
    ys = ys.transpose(1, 0, 2, 3, 4, 5).reshape(b, nc * L, h, p)[:, :s]
    return ys.astype(xh.dtype)


def gather_blocks(kv, idx):
    return jax.vmap(jax.vmap(lambda a, i: a[i]))(kv, idx)


def moba_attention(q, k, v):
    b, s, h, dh = q.shape
    bs = MOBA_BLOCK
    pad = (-s) % bs
    nb = (s + pad) // bs
    kp = jnp.pad(k, ((0, 0), (0, pad), (0, 0), (0, 0)))
    vp = jnp.pad(v, ((0, 0), (0, pad), (0, 0), (0, 0)))
    kt = kp.transpose(0, 2, 1, 3).reshape(b, h, nb, bs, dh)
    vt = vp.transpose(0, 2, 1, 3).reshape(b, h, nb, bs, dh)
    k_mean = jnp.mean(kt.astype(jnp.float32), axis=3)
    topk = min(MOBA_TOPK, nb)
    scale = dh ** -0.5
    nq = s // Q_CHUNK
    qc = q.transpose(0, 2, 1, 3).reshape(b, h, nq, Q_CHUNK, dh).transpose(2, 0, 1, 3, 4)
    starts = jnp.arange(nq, dtype=jnp.int32) * Q_CHUNK
    blk_ids = jnp.arange(nb, dtype=jnp.int32)

    def one_chunk(args):
        q_, t0 = args
        blk = t0 // bs
        qpos = t0 + jnp.arange(Q_CHUNK, dtype=jnp.int32)
        gate = jnp.einsum('bhqd,bhnd->bhqn', q_.astype(jnp.float32), k_mean)
        gate = jnp.where(blk_ids < blk, gate, NEG)
        _, idx = lax.top_k(gate, topk)
        valid = jnp.arange(topk, dtype=jnp.int32) < blk
        idx = jnp.where(valid, idx, 0)
        k_sel = gather_blocks(kt, idx)
        v_sel = gather_blocks(vt, idx)
        s_sel = jnp.einsum('bhqd,bhqjtd->bhqjt', q_, k_sel).astype(jnp.float32) * scale
        s_sel = jnp.where(valid[:, None], s_sel, NEG)
        k_own = lax.dynamic_index_in_dim(kt, blk, axis=2, keepdims=False)
        v_own = lax.dynamic_index_in_dim(vt, blk, axis=2, keepdims=False)
        s_own = jnp.einsum('bhqd,bhtd->bhqt', q_, k_own).astype(jnp.float32) * scale
        kpos = blk * bs + jnp.arange(bs, dtype=jnp.int32)
        s_own = jnp.where(kpos[None, :] <= qpos[:, None], s_own, NEG)
        scores = jnp.concatenate([s_sel.reshape(b, h, Q_CHUNK, topk * bs), s_own], axis=-1)
        pr = jax.nn.softmax(scores, axis=-1)
        p_sel = pr[..., :topk * bs].reshape(b, h, Q_CHUNK, topk, bs).astype(v.dtype)
        p_own = pr[..., topk * bs:].astype(v.dtype)
        return jnp.einsum('bhqjt,bhqjtd->bhqd', p_sel, v_sel) + jnp.einsum('bhqt,bhtd->bhqd', p_own, v_own)

    o = lax.map(one_chunk, (qc, starts))
    return o.transpose(1, 0, 3, 2, 4).reshape(b, s, h * dh)


def setup_inputs(seed: int = 0) -> dict:
    key = jax.random.key(seed)
    ks = jax.random.split(key, 16)
    f32 = jnp.float32
    x = jax.random.normal(ks[0], (BATCH, SEQ, D_MODEL), f32)
    norm_w = 1.0 + 0.02 * jax.random.normal(ks[1], (DEPTH, D_MODEL), f32)
    w_in = jax.random.normal(ks[2], (DEPTH, D_MODEL, IN_COLS), f32) * D_MODEL ** -0.5
    conv_w = jax.random.normal(ks[3], (DEPTH, SSM_CONV, SSM_CONV_DIM), f32) * SSM_CONV ** -0.5
    conv_b = 0.02 * jax.random.normal(ks[4], (DEPTH, SSM_CONV_DIM), f32)
    dt0 = jnp.exp(jax.random.uniform(ks[5], (DEPTH, SSM_HEADS), f32, math.log(1e-3), math.log(1e-1)))
    dt_bias = dt0 + jnp.log(-jnp.expm1(-dt0))
    A_log = jnp.log(jax.random.uniform(ks[6], (DEPTH, SSM_HEADS), f32, 1.0, 16.0))
    D_skip = 1.0 + 0.1 * jax.random.normal(ks[7], (DEPTH, SSM_HEADS), f32)
    ssm_norm_w = 1.0 + 0.02 * jax.random.normal(ks[8], (DEPTH, SSM_D_INNER), f32)
    w_ssm_proj = jax.random.normal(ks[9], (DEPTH, SSM_D_INNER, D_MODEL), f32) * SSM_D_INNER ** -0.5
    w_attn_proj = jax.random.normal(ks[10], (DEPTH, ATTN_D, D_MODEL), f32) * ATTN_D ** -0.5
    gate_bias = 0.02 * jax.random.normal(ks[11], (DEPTH, N_BRANCH * D_MODEL), f32)
    w_out = jax.random.normal(ks[12], (DEPTH, D_MODEL, D_MODEL), f32) * D_MODEL ** -0.5
    final_norm_w = 1.0 + 0.02 * jax.random.normal(ks[13], (D_MODEL,), f32)
    return {"x": x, "norm_w": norm_w, "w_in": w_in, "conv_w": conv_w, "conv_b": conv_b,
            "dt_bias": dt_bias, "A_log": A_log, "D_skip": D_skip, "ssm_norm_w": ssm_norm_w,
            "w_ssm_proj": w_ssm_proj, "w_attn_proj": w_attn_proj, "gate_bias": gate_bias,
            "w_out": w_out, "final_norm_w": final_norm_w}


def reference(x, norm_w, w_in, conv_w, conv_b, dt_bias, A_log, D_skip, ssm_norm_w,
              w_ssm_proj, w_attn_proj, gate_bias, w_out, final_norm_w):
    b, s, _ = x.shape
    split_pts = list(np.cumsum(COL_SIZES)[:-1])
    xbc_pts = [SSM_D_INNER, SSM_D_INNER + SSM_GROUPS * SSM_STATE]
    for l in range(DEPTH):
        h = rmsnorm(x, norm_w[l])
        proj = h @ w_in[l]
        z, xbc, dt_raw, q, k, v, g_attn, g_merge = jnp.split(proj, split_pts, axis=-1)

        xbc = jax.nn.silu(causal_dwconv(xbc, conv_w[l], conv_b[l]))
        xs, Bm, Cm = jnp.split(xbc, xbc_pts, axis=-1)
        xh = xs.reshape(b, s, SSM_HEADS, SSM_HEAD_DIM)
        dt = jax.nn.softplus(dt_raw.astype(jnp.float32) + dt_bias[l].astype(jnp.float32))
        A = -jnp.exp(A_log[l].astype(jnp.float32))
        y = ssd_scan(xh, dt, A,
                     Bm.reshape(b, s, SSM_GROUPS, SSM_STATE), Cm.reshape(b, s, SSM_GROUPS, SSM_STATE))
        y = y + D_skip[l].astype(y.dtype)[:, None] * xh
        y = y.reshape(b, s, SSM_D_INNER) * jax.nn.silu(z)
        y = group_rmsnorm(y, ssm_norm_w[l], SSM_GROUPS)
        y_ssm = y @ w_ssm_proj[l]

        o = moba_attention(q.reshape(b, s, ATTN_HEADS, ATTN_HEAD_DIM),
                           k.reshape(b, s, ATTN_HEADS, ATTN_HEAD_DIM),
                           v.reshape(b, s, ATTN_HEADS, ATTN_HEAD_DIM))
        y_attn = (o * jax.nn.silu(g_attn)) @ w_attn_proj[l]

        gates = jax.nn.sigmoid(g_merge + gate_bias[l].astype(g_merge.dtype))
        g_ssm, g_att = jnp.split(gates, 2, axis=-1)
        mixed = g_ssm * y_ssm + g_att * y_attn
        x = x + mixed @ w_out[l]
    return rmsnorm(x, final_norm_w)
```

```python
import functools

import jax
import jax.numpy as jnp
from jax import lax
from jax.experimental import pallas as pl
from jax.experimental.pallas import tpu as pltpu

F32 = jnp.float32
BF16 = jnp.bfloat16

D_MODEL = 2048
D_INNER = 4096
HEAD_DIM = 64
N_HEADS = 64
N_GROUPS = 8
HEADS_PER_GROUP = N_HEADS // N_GROUPS
GROUP_W = D_INNER // N_GROUPS
N_STATE = 128
CONV_K = 4
CHUNK = 256
ATTN_HEADS = 16
ATTN_DH = 128
MOBA_BLOCK = 256
MOBA_TOPK = 3
EPS = 1e-6
NEG = -1e30

OFF_Z = 0
OFF_X = 4096
OFF_B = 8192
OFF_C = 9216
OFF_Q = 10240
OFF_K = 12288
OFF_V = 14336
OFF_GA = 16384
OFF_GM = 18432
PROJ_W = 22528

LANES = 128
SUBLANES = 8
VMEM_LIMIT = 56 * 1024 * 1024


def _dot(a, b):
    return jnp.dot(a, b, preferred_element_type=F32)


def _dot_nt(a, b):
    return lax.dot_general(a, b, (((1,), (1,)), ((), ())), preferred_element_type=F32)


def _dot_tn(a, b):
    return lax.dot_general(a, b, (((0,), (0,)), ((), ())), preferred_element_type=F32)


def _split3(a):
    a1 = a.astype(BF16)
    r1 = a - a1.astype(F32)
    a2 = r1.astype(BF16)
    a3 = (r1 - a2.astype(F32)).astype(BF16)
    return a1, a2, a3


def _sigmoid(x):
    return 1.0 / (1.0 + jnp.exp(-x))


def _silu(x):
    return x * _sigmoid(x)


IN_TM = 1024
IN_TN = 1024


def _inproj_kernel(x_ref, nw_ref, w_ref, wdh_ref, wdl_ref, o_ref, dt_ref, h_ref):
    j = pl.program_id(1)

    @pl.when(j == 0)
    def _():
        x = x_ref[...]
        ms = jnp.mean(x * x, axis=-1, keepdims=True)
        h = x * lax.rsqrt(ms + EPS) * nw_ref[...]
        hb = h.astype(BF16)
        h_ref[...] = hb
        hl = (h - hb.astype(F32)).astype(BF16)
        wdh = wdh_ref[...]
        dt_ref[...] = _dot(hb, wdh) + _dot(hb, wdl_ref[...]) + _dot(hl, wdh)

    o_ref[...] = _dot(h_ref[...], w_ref[...]).astype(o_ref.dtype)


def _in_projection(x2, norm_w, w_main, wdt_hi, wdt_lo):
    t = x2.shape[0]
    grid = (t // IN_TM, PROJ_W // IN_TN)
    return pl.pallas_call(
        _inproj_kernel,
        grid=grid,
        in_specs=[
            pl.BlockSpec((IN_TM, D_MODEL), lambda i, j: (i, 0)),
            pl.BlockSpec((1, D_MODEL), lambda i, j: (0, 0)),
            pl.BlockSpec((D_MODEL, IN_TN), lambda i, j: (0, j)),
            pl.BlockSpec((D_MODEL, LANES), lambda i, j: (0, 0)),
            pl.BlockSpec((D_MODEL, LANES), lambda i, j: (0, 0)),
        ],
        out_specs=[
            pl.BlockSpec((IN_TM, IN_TN), lambda i, j: (i, j)),
            pl.BlockSpec((IN_TM, LANES), lambda i, j: (i, 0)),
        ],
        out_shape=[
            jax.ShapeDtypeStruct((t, PROJ_W), BF16),
            jax.ShapeDtypeStruct((t, LANES), F32),
        ],
        scratch_shapes=[pltpu.VMEM((IN_TM, D_MODEL), BF16)],
        compiler_params=pltpu.CompilerParams(
            dimension_semantics=("arbitrary", "arbitrary"), vmem_limit_bytes=VMEM_LIMIT),
        name="in_projection",
    )(x2, norm_w, w_main, wdt_hi, wdt_lo)


PADROWS = SUBLANES + CHUNK


def _ssd_kernel(z_ref, x_ref, b_ref, c_ref, dtraw_ref,
                cwx_ref, cbx_ref, cwb_ref, cbb_ref, cwc_ref, cbc_ref,
                dtb_ref, alog_ref, dskip_ref, nw_ref, e_ref,
                y_ref,
                state_ref, tail_ref, work_ref, cumg_ref, cumt_ref, ex_ref, dl_ref):
    c = pl.program_id(1)
    g = pl.program_id(2)

    row = lax.broadcasted_iota(jnp.int32, (CHUNK, CHUNK), 0)
    col = lax.broadcasted_iota(jnp.int32, (CHUNK, CHUNK), 1)
    causal = row >= col

    @pl.when(g == 0)
    def _():
        dtr = dtraw_ref[...] + dtb_ref[...]
        dt = jnp.maximum(dtr, 0.0) + jnp.log1p(jnp.exp(-jnp.abs(dtr)))
        a = dt * (-jnp.exp(alog_ref[...]))
        a1, a2, a3 = _split3(a)
        tril = causal.astype(BF16)
        cum = _dot(tril, a1) + _dot(tril, a2) + _dot(tril, a3)
        cumt_ref[...] = cum.T
        for gg in range(N_GROUPS):
            sh = (LANES - HEADS_PER_GROUP * gg) % LANES
            cumg_ref[gg] = cum if sh == 0 else pltpu.roll(cum, sh, axis=1)
        last = cum[CHUNK - 1:CHUNK, :]
        e = e_ref[...]
        quantities = (dt, jnp.exp(cum), jnp.exp(last - cum))
        for k in range(3):
            ek = _dot(quantities[k].astype(BF16), e)
            for gg in range(N_GROUPS):
                ex_ref[k, gg] = ek[:, GROUP_W * gg:GROUP_W * (gg + 1)].astype(BF16)
        dlast = jnp.broadcast_to(jnp.exp(last), (SUBLANES, LANES))
        dh = dlast.astype(BF16)
        dlo = (dlast - dh.astype(F32)).astype(BF16)
        dl = _dot(dh, e) + _dot(dlo, e)
        for gg in range(N_GROUPS):
            dl_ref[gg] = dl[:, GROUP_W * gg:GROUP_W * (gg + 1)]

    @pl.when(c == 0)
    def _():
        tail_ref[g] = jnp.zeros((SUBLANES, GROUP_W + 2 * N_STATE), F32)
        state_ref[g] = jnp.zeros((N_STATE, GROUP_W), F32)

    work_ref[0:SUBLANES, :] = tail_ref[g]
    work_ref[SUBLANES:PADROWS, 0:GROUP_W] = x_ref[...].astype(F32)
    work_ref[SUBLANES:PADROWS, GROUP_W:GROUP_W + N_STATE] = b_ref[...].astype(F32)
    work_ref[SUBLANES:PADROWS, GROUP_W + N_STATE:GROUP_W + 2 * N_STATE] = c_ref[...].astype(F32)
    tail_ref[g] = work_ref[CHUNK:PADROWS, :]

    def conv(lo, hi, w_ref, bias_ref):
        acc = bias_ref[...]
        for k in range(CONV_K):
            s0 = SUBLANES - (CONV_K - 1) + k
            acc = acc + w_ref[k:k + 1, :] * work_ref[s0:s0 + CHUNK, lo:hi]
        return _silu(acc)

    xc = conv(0, GROUP_W, cwx_ref, cbx_ref)
    bmat = conv(GROUP_W, GROUP_W + N_STATE, cwb_ref, cbb_ref).astype(BF16)
    cmat = conv(GROUP_W + N_STATE, GROUP_W + 2 * N_STATE, cwc_ref, cbc_ref).astype(BF16)

    cb = _dot_nt(cmat, bmat)
    e_dt = ex_ref[0, g].astype(F32)
    e_cum = ex_ref[1, g].astype(F32)
    e_end = ex_ref[2, g].astype(F32)
    xdt = xc * e_dt
    xdt_b = xdt.astype(BF16)
    xde_b = (xdt * e_end).astype(BF16)

    cg = cumg_ref[g]
    lane = lax.broadcasted_iota(jnp.int32, (CHUNK, LANES), 1)
    ys = []
    for pair in range(HEADS_PER_GROUP // 2):
        xp = xdt_b[:, LANES * pair:LANES * (pair + 1)]
        acc = None
        for sub in range(2):
            hh = 2 * pair + sub
            colv = cg[:, hh:hh + 1]
            rowv = cumt_ref[pl.ds(g * HEADS_PER_GROUP + hh, 1), :]
            seg = jnp.where(causal, colv - rowv, NEG)
            w = (cb * jnp.exp(seg)).astype(BF16)
            in_head = (lane >= HEAD_DIM * sub) & (lane < HEAD_DIM * (sub + 1))
            rhs = jnp.where(in_head, xp, jnp.zeros_like(xp))
            t = _dot(w, rhs)
            acc = t if acc is None else acc + t
        ys.append(acc)
    y = jnp.concatenate(ys, axis=1)

    st = state_ref[g]
    y = y + _dot(cmat, st.astype(BF16)) * e_cum
    y = y + dskip_ref[...] * xc
    y = y * _silu(z_ref[...].astype(F32))
    ms = jnp.mean(y * y, axis=-1, keepdims=True)
    y_ref[...] = (y * lax.rsqrt(ms + EPS) * nw_ref[...]).astype(y_ref.dtype)

    state_ref[g] = st * dl_ref[g][0:1, :] + _dot_tn(bmat, xde_b)


def _ssd_branch(proj, dt_raw, conv_w, conv_b, dtb_pad, alog_pad, dskip_exp, ssm_norm_w, expand, batch, seq):
    t = proj.shape[0]
    nc = seq // CHUNK
    xw = GROUP_W // LANES

    def rows(b, c, g):
        return b * nc + c

    in_specs = [
        pl.BlockSpec((CHUNK, GROUP_W), lambda b, c, g: (rows(b, c, g), OFF_Z // GROUP_W + g)),
        pl.BlockSpec((CHUNK, GROUP_W), lambda b, c, g: (rows(b, c, g), OFF_X // GROUP_W + g)),
        pl.BlockSpec((CHUNK, N_STATE), lambda b, c, g: (rows(b, c, g), OFF_B // N_STATE + g)),
        pl.BlockSpec((CHUNK, N_STATE), lambda b, c, g: (rows(b, c, g), OFF_C // N_STATE + g)),
        pl.BlockSpec((CHUNK, LANES), lambda b, c, g: (rows(b, c, g), 0)),
        pl.BlockSpec((CONV_K, GROUP_W), lambda b, c, g: (0, g)),
        pl.BlockSpec((1, GROUP_W), lambda b, c, g: (0, g)),
        pl.BlockSpec((CONV_K, N_STATE), lambda b, c, g: (0, D_INNER // N_STATE + g)),
        pl.BlockSpec((1, N_STATE), lambda b, c, g: (0, D_INNER // N_STATE + g)),
        pl.BlockSpec((CONV_K, N_STATE), lambda b, c, g: (0, D_INNER // N_STATE + N_GROUPS + g)),
        pl.BlockSpec((1, N_STATE), lambda b, c, g: (0, D_INNER // N_STATE + N_GROUPS + g)),
        pl.BlockSpec((1, LANES), lambda b, c, g: (0, 0)),
        pl.BlockSpec((1, LANES), lambda b, c, g: (0, 0)),
        pl.BlockSpec((1, GROUP_W), lambda b, c, g: (0, g)),
        pl.BlockSpec((1, GROUP_W), lambda b, c, g: (0, g)),
        pl.BlockSpec((LANES, D_INNER), lambda b, c, g: (0, 0)),
    ]
    del xw
    return pl.pallas_call(
        _ssd_kernel,
        grid=(batch, nc, N_GROUPS),
        in_specs=in_specs,
        out_specs=pl.BlockSpec((CHUNK, GROUP_W), lambda b, c, g: (rows(b, c, g), g)),
        out_shape=jax.ShapeDtypeStruct((t, D_INNER), BF16),
        scratch_shapes=[
            pltpu.VMEM((N_GROUPS, N_STATE, GROUP_W), F32),
            pltpu.VMEM((N_GROUPS, SUBLANES, GROUP_W + 2 * N_STATE), F32),
            pltpu.VMEM((PADROWS, GROUP_W + 2 * N_STATE), F32),
            pltpu.VMEM((N_GROUPS, CHUNK, LANES), F32),
            pltpu.VMEM((LANES, CHUNK), F32),
            pltpu.VMEM((3, N_GROUPS, CHUNK, GROUP_W), BF16),
            pltpu.VMEM((N_GROUPS, SUBLANES, GROUP_W), F32),
        ],
        compiler_params=pltpu.CompilerParams(
            dimension_semantics=("arbitrary", "arbitrary", "arbitrary"), vmem_limit_bytes=VMEM_LIMIT),
        name="ssd_branch",
    )(proj, proj, proj, proj, dt_raw, conv_w, conv_b, conv_w, conv_b, conv_w, conv_b,
      dtb_pad, alog_pad, dskip_exp, ssm_norm_w, expand)


def _moba_kernel(q_ref, k_ref, v_ref, g_ref, o_ref, kaug_ref, kmh_ref, kml_ref):
    i = pl.program_id(2)
    seq = k_ref.shape[0]
    nb = seq // MOBA_BLOCK
    scale = ATTN_DH ** -0.5

    @pl.when(i == 0)
    def _():
        k = k_ref[...]
        kaug_ref[:, 0:ATTN_DH] = k
        r = lax.broadcasted_iota(jnp.int32, (seq, LANES), 0) // MOBA_BLOCK
        cidx = lax.broadcasted_iota(jnp.int32, (seq, LANES), 1)
        kaug_ref[:, ATTN_DH:ATTN_DH + LANES] = (r == cidx).astype(BF16)
        km = jnp.mean(k.astype(F32).reshape(nb, MOBA_BLOCK, ATTN_DH), axis=1)
        kh = km.astype(BF16)
        kmh_ref[...] = jnp.zeros((LANES, ATTN_DH), BF16)
        kml_ref[...] = jnp.zeros((LANES, ATTN_DH), BF16)
        kmh_ref[0:nb, :] = kh
        kml_ref[0:nb, :] = (km - kh.astype(F32)).astype(BF16)

    q = q_ref[...]
    gate = _dot_nt(q, kmh_ref[...]) + _dot_nt(q, kml_ref[...])
    lane = lax.broadcasted_iota(jnp.int32, (MOBA_BLOCK, LANES), 1)
    past = lane < i
    gate = jnp.where(past, gate, NEG)
    sel = jnp.zeros((MOBA_BLOCK, LANES), jnp.bool_)
    for _ in range(MOBA_TOPK):
        mx = jnp.max(gate, axis=1, keepdims=True)
        idx = jnp.min(jnp.where(gate == mx, lane, 4 * LANES), axis=1, keepdims=True)
        hit = lane == idx
        sel = sel | (hit & past)
        gate = jnp.where(hit, -jnp.inf, gate)
    bias = jnp.where(sel, 0.0, NEG).astype(BF16)
    qa = jnp.concatenate([q, bias], axis=1)

    own = pl.multiple_of(i * MOBA_BLOCK, MOBA_BLOCK)
    s = _dot_nt(q, k_ref[pl.ds(own, MOBA_BLOCK), :]) * scale
    rr = lax.broadcasted_iota(jnp.int32, (MOBA_BLOCK, MOBA_BLOCK), 0)
    cc = lax.broadcasted_iota(jnp.int32, (MOBA_BLOCK, MOBA_BLOCK), 1)
    s = jnp.where(cc <= rr, s, NEG)
    m = jnp.max(s, axis=1, keepdims=True)
    p = jnp.exp(s - m)
    l = jnp.sum(p, axis=1, keepdims=True)
    acc = _dot(p.astype(BF16), v_ref[pl.ds(own, MOBA_BLOCK), :])

    def body(j, carry):
        m, l, acc = carry
        off = pl.multiple_of(j * MOBA_BLOCK, MOBA_BLOCK)
        s = _dot_nt(qa, kaug_ref[pl.ds(off, MOBA_BLOCK), :]) * scale
        m_new = jnp.maximum(m, jnp.max(s, axis=1, keepdims=True))
        alpha = jnp.exp(m - m_new)
        p = jnp.exp(s - m_new)
        l = alpha * l + jnp.sum(p, axis=1, keepdims=True)
        acc = alpha * acc + _dot(p.astype(BF16), v_ref[pl.ds(off, MOBA_BLOCK), :])
        return m_new, l, acc

    m, l, acc = lax.fori_loop(0, i, body, (m, l, acc))
    o_ref[...] = (acc / l * _silu(g_ref[...].astype(F32))).astype(o_ref.dtype)


def _moba_branch(proj, batch, seq):
    t = proj.shape[0]
    nb = seq // MOBA_BLOCK

    def qrow(b, h, i):
        return b * nb + i

    return pl.pallas_call(
        _moba_kernel,
        grid=(batch, ATTN_HEADS, nb),
        in_specs=[
            pl.BlockSpec((MOBA_BLOCK, ATTN_DH), lambda b, h, i: (qrow(b, h, i), OFF_Q // ATTN_DH + h)),
            pl.BlockSpec((seq, ATTN_DH), lambda b, h, i: (b, OFF_K // ATTN_DH + h)),
            pl.BlockSpec((seq, ATTN_DH), lambda b, h, i: (b, OFF_V // ATTN_DH + h)),
            pl.BlockSpec((MOBA_BLOCK, ATTN_DH), lambda b, h, i: (qrow(b, h, i), OFF_GA // ATTN_DH + h)),
        ],
        out_specs=pl.BlockSpec((MOBA_BLOCK, ATTN_DH), lambda b, h, i: (qrow(b, h, i), h)),
        out_shape=jax.ShapeDtypeStruct((t, ATTN_HEADS * ATTN_DH), BF16),
        scratch_shapes=[
            pltpu.VMEM((seq, ATTN_DH + LANES), BF16),
            pltpu.VMEM((LANES, ATTN_DH), BF16),
            pltpu.VMEM((LANES, ATTN_DH), BF16),
        ],
        compiler_params=pltpu.CompilerParams(
            dimension_semantics=("arbitrary", "arbitrary", "arbitrary"), vmem_limit_bytes=VMEM_LIMIT),
        name="moba_branch",
    )(proj, proj, proj, proj)


MIX_TM = 512
MIX_TN = 512
OUT_TM = 512


def _mix_kernel(yn_ref, og_ref, ws_ref, wa_ref, gs_ref, ga_ref, bs_ref, ba_ref, o_ref):
    y_ssm = _dot(yn_ref[...], ws_ref[...])
    y_att = _dot(og_ref[...], wa_ref[...])
    g_ssm = _sigmoid(gs_ref[...].astype(F32) + bs_ref[...])
    g_att = _sigmoid(ga_ref[...].astype(F32) + ba_ref[...])
    o_ref[...] = (g_ssm * y_ssm + g_att * y_att).astype(o_ref.dtype)


def _gated_merge(yn, og, w_ssm, w_att, proj, gate_bias):
    t = yn.shape[0]
    nj = D_MODEL // MIX_TN
    return pl.pallas_call(
        _mix_kernel,
        grid=(t // MIX_TM, nj),
        in_specs=[
            pl.BlockSpec((MIX_TM, D_INNER), lambda i, j: (i, 0)),
            pl.BlockSpec((MIX_TM, D_MODEL), lambda i, j: (i, 0)),
            pl.BlockSpec((D_INNER, MIX_TN), lambda i, j: (0, j)),
            pl.BlockSpec((D_MODEL, MIX_TN), lambda i, j: (0, j)),
            pl.BlockSpec((MIX_TM, MIX_TN), lambda i, j: (i, OFF_GM // MIX_TN + j)),
            pl.BlockSpec((MIX_TM, MIX_TN), lambda i, j: (i, OFF_GM // MIX_TN + nj + j)),
            pl.BlockSpec((1, MIX_TN), lambda i, j: (0, j)),
            pl.BlockSpec((1, MIX_TN), lambda i, j: (0, nj + j)),
        ],
        out_specs=pl.BlockSpec((MIX_TM, MIX_TN), lambda i, j: (i, j)),
        out_shape=jax.ShapeDtypeStruct((t, D_MODEL), BF16),
        compiler_params=pltpu.CompilerParams(
            dimension_semantics=("arbitrary", "arbitrary"), vmem_limit_bytes=VMEM_LIMIT),
        name="gated_merge",
    )(yn, og, w_ssm, w_att, proj, proj, gate_bias, gate_bias)


def _out_kernel(m_ref, w_ref, x_ref, fw_ref, o_ref):
    r = x_ref[...] + _dot(m_ref[...], w_ref[...])
    ms = jnp.mean(r * r, axis=-1, keepdims=True)
    o_ref[...] = r * lax.rsqrt(ms + EPS) * fw_ref[...]


def _output(mixed, w_out, x2, final_w):
    t = x2.shape[0]
    return pl.pallas_call(
        _out_kernel,
        grid=(t // OUT_TM,),
        in_specs=[
            pl.BlockSpec((OUT_TM, D_MODEL), lambda i: (i, 0)),
            pl.BlockSpec((D_MODEL, D_MODEL), lambda i: (0, 0)),
            pl.BlockSpec((OUT_TM, D_MODEL), lambda i: (i, 0)),
            pl.BlockSpec((1, D_MODEL), lambda i: (0, 0)),
        ],
        out_specs=pl.BlockSpec((OUT_TM, D_MODEL), lambda i: (i, 0)),
        out_shape=jax.ShapeDtypeStruct((t, D_MODEL), F32),
        compiler_params=pltpu.CompilerParams(
            dimension_semantics=("arbitrary",), vmem_limit_bytes=VMEM_LIMIT),
        name="output_norm",
    )(mixed, w_out, x2, final_w)


def _pad_lanes(v):
    return jnp.pad(v, (0, LANES - v.shape[0])).reshape(1, LANES)


def kernel(x, norm_w, w_in, conv_w, conv_b, dt_bias, A_log, D_skip, ssm_norm_w,
           w_ssm_proj, w_attn_proj, gate_bias, w_out, final_norm_w):
    batch, seq, _ = x.shape
    depth = norm_w.shape[0]
    dt_lo = D_INNER + (D_INNER + 2 * N_GROUPS * N_STATE)
    dt_hi = dt_lo + N_HEADS
    head_of_channel = jnp.arange(D_INNER, dtype=jnp.int32) // HEAD_DIM
    expand = (jnp.arange(LANES, dtype=jnp.int32)[:, None] == head_of_channel[None, :]).astype(BF16)

    assert depth == 1, "the block is built for a single layer"
    l = 0
    x2 = x.reshape(batch * seq, D_MODEL)
    w = w_in[l]
    w_main = jnp.concatenate([w[:, :dt_lo], w[:, dt_hi:]], axis=1).astype(BF16)
    w_dt = jnp.pad(w[:, dt_lo:dt_hi], ((0, 0), (0, LANES - N_HEADS)))
    wdt_hi = w_dt.astype(BF16)
    wdt_lo = (w_dt - wdt_hi.astype(F32)).astype(BF16)

    proj, dt_raw = _in_projection(x2, norm_w[l].reshape(1, D_MODEL), w_main, wdt_hi, wdt_lo)

    yn = _ssd_branch(
        proj, dt_raw, conv_w[l], conv_b[l].reshape(1, -1),
        _pad_lanes(dt_bias[l]), _pad_lanes(A_log[l]),
        jnp.repeat(D_skip[l], HEAD_DIM).reshape(1, D_INNER),
        ssm_norm_w[l].reshape(1, D_INNER), expand, batch, seq)

    og = _moba_branch(proj, batch, seq)

    mixed = _gated_merge(yn, og, w_ssm_proj[l].astype(BF16), w_attn_proj[l].astype(BF16),
                         proj, gate_bias[l].reshape(1, -1))
    out = _output(mixed, w_out[l].astype(BF16), x2, final_norm_w.reshape(1, D_MODEL))
    return out.reshape(batch, seq, D_MODEL)
```

```python
import functools

import jax
import jax.numpy as jnp
from jax import lax
from jax.experimental import pallas as pl
from jax.experimental.pallas import tpu as pltpu

F32 = jnp.float32
BF16 = jnp.bfloat16

D_MODEL = 2048
D_INNER = 4096
HEAD_DIM = 64
N_HEADS = 64
N_GROUPS = 8
HEADS_PER_GROUP = N_HEADS // N_GROUPS
GROUP_W = D_INNER // N_GROUPS
N_STATE = 128
CONV_K = 4
CHUNK = 256
ATTN_HEADS = 16
ATTN_DH = 128
MOBA_BLOCK = 256
MOBA_TOPK = 3
EPS = 1e-6
NEG = -1e30

OFF_Z = 0
OFF_X = 4096
OFF_B = 8192
OFF_C = 9216
OFF_Q = 10240
OFF_K = 12288
OFF_V = 14336
OFF_GA = 16384
OFF_GM = 18432
PROJ_W = 22528

LANES = 128
SUBLANES = 8
VMEM_LIMIT = 56 * 1024 * 1024


def _dot(a, b):
    return jnp.dot(a, b, preferred_element_type=F32)


def _dot_nt(a, b):
    return lax.dot_general(a, b, (((1,), (1,)), ((), ())), preferred_element_type=F32)


def _dot_tn(a, b):
    return lax.dot_general(a, b, (((0,), (0,)), ((), ())), preferred_element_type=F32)


def _split3(a):
    a1 = a.astype(BF16)
    r1 = a - a1.astype(F32)
    a2 = r1.astype(BF16)
    a3 = (r1 - a2.astype(F32)).astype(BF16)
    return a1, a2, a3


def _sigmoid(x):
    return 1.0 / (1.0 + jnp.exp(-x))


def _silu(x):
    return x * _sigmoid(x)


IN_TM = 1024
IN_TN = 1024


def _inproj_kernel(x_ref, nw_ref, w_ref, wdh_ref, wdl_ref, o_ref, dt_ref, h_ref):
    j = pl.program_id(1)

    @pl.when(j == 0)
    def _():
        x = x_ref[...]
        ms = jnp.mean(x * x, axis=-1, keepdims=True)
        h = x * lax.rsqrt(ms + EPS) * nw_ref[...]
        hb = h.astype(BF16)
        h_ref[...] = hb
        hl = (h - hb.astype(F32)).astype(BF16)
        wdh = wdh_ref[...]
        dt_ref[...] = _dot(hb, wdh) + _dot(hb, wdl_ref[...]) + _dot(hl, wdh)

    o_ref[...] = _dot(h_ref[...], w_ref[...]).astype(o_ref.dtype)


def _in_projection(x2, norm_w, w_main, wdt_hi, wdt_lo):
    t = x2.shape[0]
    grid = (t // IN_TM, PROJ_W // IN_TN)
    return pl.pallas_call(
        _inproj_kernel,
        grid=grid,
        in_specs=[
            pl.BlockSpec((IN_TM, D_MODEL), lambda i, j: (i, 0)),
            pl.BlockSpec((1, D_MODEL), lambda i, j: (0, 0)),
            pl.BlockSpec((D_MODEL, IN_TN), lambda i, j: (0, j)),
            pl.BlockSpec((D_MODEL, LANES), lambda i, j: (0, 0)),
            pl.BlockSpec((D_MODEL, LANES), lambda i, j: (0, 0)),
        ],
        out_specs=[
            pl.BlockSpec((IN_TM, IN_TN), lambda i, j: (i, j)),
            pl.BlockSpec((IN_TM, LANES), lambda i, j: (i, 0)),
        ],
        out_shape=[
            jax.ShapeDtypeStruct((t, PROJ_W), BF16),
            jax.ShapeDtypeStruct((t, LANES), F32),
        ],
        scratch_shapes=[pltpu.VMEM((IN_TM, D_MODEL), BF16)],
        compiler_params=pltpu.CompilerParams(
            dimension_semantics=("arbitrary", "arbitrary"), vmem_limit_bytes=VMEM_LIMIT),
        name="in_projection",
    )(x2, norm_w, w_main, wdt_hi, wdt_lo)


PADROWS = SUBLANES + CHUNK


def _ssd_kernel(z_ref, x_ref, b_ref, c_ref, dtraw_ref,
                cwx_ref, cbx_ref, cwb_ref, cbb_ref, cwc_ref, cbc_ref,
                dtb_ref, alog_ref, dskip_ref, nw_ref, e_ref,
                y_ref,
                state_ref, tail_ref, work_ref, cumg_ref, cumt_ref, ex_ref, dl_ref):
    c = pl.program_id(1)
    g = pl.program_id(2)

    row = lax.broadcasted_iota(jnp.int32, (CHUNK, CHUNK), 0)
    col = lax.broadcasted_iota(jnp.int32, (CHUNK, CHUNK), 1)
    causal = row >= col

    @pl.when(g == 0)
    def _():
        dtr = dtraw_ref[...] + dtb_ref[...]
        dt = jnp.maximum(dtr, 0.0) + jnp.log1p(jnp.exp(-jnp.abs(dtr)))
        a = dt * (-jnp.exp(alog_ref[...]))
        a1, a2, a3 = _split3(a)
        tril = causal.astype(BF16)
        cum = _dot(tril, a1) + _dot(tril, a2) + _dot(tril, a3)
        cumt_ref[...] = cum.T
        for gg in range(N_GROUPS):
            sh = (LANES - HEADS_PER_GROUP * gg) % LANES
            cumg_ref[gg] = cum if sh == 0 else pltpu.roll(cum, sh, axis=1)
        last = cum[CHUNK - 1:CHUNK, :]
        e = e_ref[...]
        quantities = (dt, jnp.exp(cum), jnp.exp(last - cum))
        for k in range(3):
            ek = _dot(quantities[k].astype(BF16), e)
            for gg in range(N_GROUPS):
                ex_ref[k, gg] = ek[:, GROUP_W * gg:GROUP_W * (gg + 1)].astype(BF16)
        dlast = jnp.broadcast_to(jnp.exp(last), (SUBLANES, LANES))
        dh = dlast.astype(BF16)
        dlo = (dlast - dh.astype(F32)).astype(BF16)
        dl = _dot(dh, e) + _dot(dlo, e)
        for gg in range(N_GROUPS):
            dl_ref[gg] = dl[:, GROUP_W * gg:GROUP_W * (gg + 1)]

    @pl.when(c == 0)
    def _():
        tail_ref[g] = jnp.zeros((SUBLANES, GROUP_W + 2 * N_STATE), F32)
        state_ref[g] = jnp.zeros((N_STATE, GROUP_W), F32)

    work_ref[0:SUBLANES, :] = tail_ref[g]
    work_ref[SUBLANES:PADROWS, 0:GROUP_W] = x_ref[...].astype(F32)
    work_ref[SUBLANES:PADROWS, GROUP_W:GROUP_W + N_STATE] = b_ref[...].astype(F32)
    work_ref[SUBLANES:PADROWS, GROUP_W + N_STATE:GROUP_W + 2 * N_STATE] = c_ref[...].astype(F32)
    tail_ref[g] = work_ref[CHUNK:PADROWS, :]

    def conv(lo, hi, w_ref, bias_ref):
        acc = bias_ref[...]
        for k in range(CONV_K):
            s0 = SUBLANES - (CONV_K - 1) + k
            acc = acc + w_ref[k:k + 1, :] * work_ref[s0:s0 + CHUNK, lo:hi]
        return _silu(acc)

    xc = conv(0, GROUP_W, cwx_ref, cbx_ref)
    bmat = conv(GROUP_W, GROUP_W + N_STATE, cwb_ref, cbb_ref).astype(BF16)
    cmat = conv(GROUP_W + N_STATE, GROUP_W + 2 * N_STATE, cwc_ref, cbc_ref).astype(BF16)

    cb = _dot_nt(cmat, bmat)
    e_dt = ex_ref[0, g].astype(F32)
    e_cum = ex_ref[1, g].astype(F32)
    e_end = ex_ref[2, g].astype(F32)
    xdt = xc * e_dt
    xdt_b = xdt.astype(BF16)
    xde_b = (xdt * e_end).astype(BF16)

    cg = cumg_ref[g]
    lane = lax.broadcasted_iota(jnp.int32, (CHUNK, LANES), 1)
    ys = []
    for pair in range(HEADS_PER_GROUP // 2):
        xp = xdt_b[:, LANES * pair:LANES * (pair + 1)]
        acc = None
        for sub in range(2):
            hh = 2 * pair + sub
            colv = cg[:, hh:hh + 1]
            rowv = cumt_ref[pl.ds(g * HEADS_PER_GROUP + hh, 1), :]
            seg = jnp.where(causal, colv - rowv, NEG)
            w = (cb * jnp.exp(seg)).astype(BF16)
            in_head = (lane >= HEAD_DIM * sub) & (lane < HEAD_DIM * (sub + 1))
            rhs = jnp.where(in_head, xp, jnp.zeros_like(xp))
            t = _dot(w, rhs)
            acc = t if acc is None else acc + t
        ys.append(acc)
    y = jnp.concatenate(ys, axis=1)

    st = state_ref[g]
    y = y + _dot(cmat, st.astype(BF16)) * e_cum
    y = y + dskip_ref[...] * xc
    y = y * _silu(z_ref[...].astype(F32))
    ms = jnp.mean(y * y, axis=-1, keepdims=True)
    y_ref[...] = (y * lax.rsqrt(ms + EPS) * nw_ref[...]).astype(y_ref.dtype)

    state_ref[g] = st * dl_ref[g][0:1, :] + _dot_tn(bmat, xde_b)


def _ssd_branch(proj, dt_raw, conv_w, conv_b, dtb_pad, alog_pad, dskip_exp, ssm_norm_w, expand, batch, seq):
    t = proj.shape[0]
    nc = seq // CHUNK
    xw = GROUP_W // LANES

    def rows(b, c, g):
        return b * nc + c

    in_specs = [
        pl.BlockSpec((CHUNK, GROUP_W), lambda b, c, g: (rows(b, c, g), OFF_Z // GROUP_W + g)),
        pl.BlockSpec((CHUNK, GROUP_W), lambda b, c, g: (rows(b, c, g), OFF_X // GROUP_W + g)),
        pl.BlockSpec((CHUNK, N_STATE), lambda b, c, g: (rows(b, c, g), OFF_B // N_STATE + g)),
        pl.BlockSpec((CHUNK, N_STATE), lambda b, c, g: (rows(b, c, g), OFF_C // N_STATE + g)),
        pl.BlockSpec((CHUNK, LANES), lambda b, c, g: (rows(b, c, g), 0)),
        pl.BlockSpec((CONV_K, GROUP_W), lambda b, c, g: (0, g)),
        pl.BlockSpec((1, GROUP_W), lambda b, c, g: (0, g)),
        pl.BlockSpec((CONV_K, N_STATE), lambda b, c, g: (0, D_INNER // N_STATE + g)),
        pl.BlockSpec((1, N_STATE), lambda b, c, g: (0, D_INNER // N_STATE + g)),
        pl.BlockSpec((CONV_K, N_STATE), lambda b, c, g: (0, D_INNER // N_STATE + N_GROUPS + g)),
        pl.BlockSpec((1, N_STATE), lambda b, c, g: (0, D_INNER // N_STATE + N_GROUPS + g)),
        pl.BlockSpec((1, LANES), lambda b, c, g: (0, 0)),
        pl.BlockSpec((1, LANES), lambda b, c, g: (0, 0)),
        pl.BlockSpec((1, GROUP_W), lambda b, c, g: (0, g)),
        pl.BlockSpec((1, GROUP_W), lambda b, c, g: (0, g)),
        pl.BlockSpec((LANES, D_INNER), lambda b, c, g: (0, 0)),
    ]
    del xw
    return pl.pallas_call(
        _ssd_kernel,
        grid=(batch, nc, N_GROUPS),
        in_specs=in_specs,
        out_specs=pl.BlockSpec((CHUNK, GROUP_W), lambda b, c, g: (rows(b, c, g), g)),
        out_shape=jax.ShapeDtypeStruct((t, D_INNER), BF16),
        scratch_shapes=[
            pltpu.VMEM((N_GROUPS, N_STATE, GROUP_W), F32),
            pltpu.VMEM((N_GROUPS, SUBLANES, GROUP_W + 2 * N_STATE), F32),
            pltpu.VMEM((PADROWS, GROUP_W + 2 * N_STATE), F32),
            pltpu.VMEM((N_GROUPS, CHUNK, LANES), F32),
            pltpu.VMEM((LANES, CHUNK), F32),
            pltpu.VMEM((3, N_GROUPS, CHUNK, GROUP_W), BF16),
            pltpu.VMEM((N_GROUPS, SUBLANES, GROUP_W), F32),
        ],
        compiler_params=pltpu.CompilerParams(
            dimension_semantics=("arbitrary", "arbitrary", "arbitrary"), vmem_limit_bytes=VMEM_LIMIT),
        name="ssd_branch",
    )(proj, proj, proj, proj, dt_raw, conv_w, conv_b, conv_w, conv_b, conv_w, conv_b,
      dtb_pad, alog_pad, dskip_exp, ssm_norm_w, expand)


HEADS_PER_STEP = 4
PAIR_W = HEADS_PER_STEP * ATTN_DH
KV_PER_ITER = 2
EXP2_SCALE = (ATTN_DH ** -0.5) * 1.4426950408889634


def _moba_kernel(q_ref, k_ref, v_ref, g_ref, o_ref,
                 vt_ref, kmean_ref, bias_ref, sa_ref, mxa_ref, sb_ref, mxb_ref):
    i = pl.program_id(2)
    seq = k_ref.shape[0]
    nb = seq // MOBA_BLOCK

    @pl.when(i == 0)
    def _():
        def prep(jb, carry):
            off = pl.multiple_of(jb * MOBA_BLOCK, MOBA_BLOCK)
            vt_ref[jb] = v_ref[pl.ds(off, MOBA_BLOCK), :].astype(F32).T.astype(BF16)
            kb = k_ref[pl.ds(off, MOBA_BLOCK), :].astype(F32)
            kmean_ref[pl.ds(jb, 1), :] = jnp.mean(kb, axis=0, keepdims=True)
            return carry
        lax.fori_loop(0, nb, prep, 0)

    km = kmean_ref[...]
    km_hi = km.astype(BF16)
    km_lo = (km - km_hi.astype(F32)).astype(BF16)
    blk = lax.broadcasted_iota(jnp.int32, (nb, MOBA_BLOCK), 0)
    past = blk < i
    key = lax.broadcasted_iota(jnp.int32, (MOBA_BLOCK, MOBA_BLOCK), 0)
    qry = lax.broadcasted_iota(jnp.int32, (MOBA_BLOCK, MOBA_BLOCK), 1)
    own = pl.multiple_of(i * MOBA_BLOCK, MOBA_BLOCK)

    qs = []
    carry = []
    for a in range(HEADS_PER_STEP):
        lo, hi = ATTN_DH * a, ATTN_DH * (a + 1)
        q = q_ref[:, lo:hi]
        qs.append(q)
        gate = _dot_nt(km_hi[:, lo:hi], q) + _dot_nt(km_lo[:, lo:hi], q)
        gate = jnp.where(past, gate, NEG)
        sel = jnp.zeros((nb, MOBA_BLOCK), jnp.bool_)
        for _ in range(MOBA_TOPK):
            mx = jnp.max(gate, axis=0, keepdims=True)
            idx = jnp.min(jnp.where(gate == mx, blk, 4 * nb), axis=0, keepdims=True)
            hit = blk == idx
            sel = sel | (hit & past)
            gate = jnp.where(hit, -jnp.inf, gate)
        bias_ref[a] = jnp.where(sel, 0.0, NEG)
        s = _dot_nt(k_ref[pl.ds(own, MOBA_BLOCK), lo:hi], q)
        s = jnp.where(key <= qry, s, NEG)
        m = jnp.max(s, axis=0, keepdims=True)
        p = jnp.exp2((s - m) * EXP2_SCALE)
        l = jnp.sum(p, axis=0, keepdims=True)
        acc = _dot(vt_ref[i, lo:hi, :], p.astype(BF16))
        carry += [m, l, acc]

    n_groups = nb // KV_PER_ITER

    def score_stage(grp, s_ref, mx_ref):
        for a in range(HEADS_PER_STEP):
            lo, hi = ATTN_DH * a, ATTN_DH * (a + 1)
            for u in range(KV_PER_ITER):
                j = KV_PER_ITER * grp + u
                off = pl.multiple_of(j * MOBA_BLOCK, MOBA_BLOCK)
                s = _dot_nt(k_ref[pl.ds(off, MOBA_BLOCK), lo:hi], qs[a])
                s = s + bias_ref[a, pl.ds(j, 1), :]
                s_ref[a, u] = s
                r = a * KV_PER_ITER + u
                mx_ref[r:r + 1, :] = jnp.max(s, axis=0, keepdims=True)

    def softmax_stage(grp, s_ref, mx_ref, carry):
        out = []
        for a in range(HEADS_PER_STEP):
            lo, hi = ATTN_DH * a, ATTN_DH * (a + 1)
            m, l, acc = carry[3 * a:3 * a + 3]
            mx = jnp.max(mx_ref[a * KV_PER_ITER:(a + 1) * KV_PER_ITER, :], axis=0, keepdims=True)
            m_new = jnp.maximum(m, mx)
            alpha = jnp.exp2((m - m_new) * EXP2_SCALE)
            l = alpha * l
            acc = alpha * acc
            for u in range(KV_PER_ITER):
                p = jnp.exp2((s_ref[a, u] - m_new) * EXP2_SCALE)
                l = l + jnp.sum(p, axis=0, keepdims=True)
                acc = acc + _dot(vt_ref[KV_PER_ITER * grp + u, lo:hi, :], p.astype(BF16))
            out += [m_new, l, acc]
        return out

    def half_step(grp, cur, nxt, carry):
        score_stage(jnp.minimum(grp + 1, n_groups - 1), *nxt)
        return softmax_stage(grp, *cur, carry)

    buf_a = (sa_ref, mxa_ref)
    buf_b = (sb_ref, mxb_ref)

    def body(tt, carry):
        carry = half_step(2 * tt, buf_a, buf_b, list(carry))
        carry = half_step(2 * tt + 1, buf_b, buf_a, carry)
        return tuple(carry)

    n_iter = (i + KV_PER_ITER - 1) // KV_PER_ITER
    score_stage(0, *buf_a)
    carry = lax.fori_loop(0, (n_iter + 1) // 2, body, tuple(carry))

    for a in range(HEADS_PER_STEP):
        lo, hi = ATTN_DH * a, ATTN_DH * (a + 1)
        m, l, acc = carry[3 * a:3 * a + 3]
        o = (acc * (1.0 / l)).T
        o_ref[:, lo:hi] = (o * _silu(g_ref[:, lo:hi].astype(F32))).astype(o_ref.dtype)


def _moba_branch(proj, batch, seq):
    t = proj.shape[0]
    nb = seq // MOBA_BLOCK
    assert nb % KV_PER_ITER == 0

    def qrow(b, h, i):
        return b * nb + i

    return pl.pallas_call(
        _moba_kernel,
        grid=(batch, ATTN_HEADS // HEADS_PER_STEP, nb),
        in_specs=[
            pl.BlockSpec((MOBA_BLOCK, PAIR_W), lambda b, h, i: (qrow(b, h, i), OFF_Q // PAIR_W + h)),
            pl.BlockSpec((seq, PAIR_W), lambda b, h, i: (b, OFF_K // PAIR_W + h)),
            pl.BlockSpec((seq, PAIR_W), lambda b, h, i: (b, OFF_V // PAIR_W + h)),
            pl.BlockSpec((MOBA_BLOCK, PAIR_W), lambda b, h, i: (qrow(b, h, i), OFF_GA // PAIR_W + h)),
        ],
        out_specs=pl.BlockSpec((MOBA_BLOCK, PAIR_W), lambda b, h, i: (qrow(b, h, i), h)),
        out_shape=jax.ShapeDtypeStruct((t, ATTN_HEADS * ATTN_DH), BF16),
        scratch_shapes=[
            pltpu.VMEM((nb, PAIR_W, MOBA_BLOCK), BF16),
            pltpu.VMEM((nb, PAIR_W), F32),
            pltpu.VMEM((HEADS_PER_STEP, nb, MOBA_BLOCK), F32),
            pltpu.VMEM((HEADS_PER_STEP, KV_PER_ITER, MOBA_BLOCK, MOBA_BLOCK), F32),
            pltpu.VMEM((HEADS_PER_STEP * KV_PER_ITER, MOBA_BLOCK), F32),
            pltpu.VMEM((HEADS_PER_STEP, KV_PER_ITER, MOBA_BLOCK, MOBA_BLOCK), F32),
            pltpu.VMEM((HEADS_PER_STEP * KV_PER_ITER, MOBA_BLOCK), F32),
        ],
        compiler_params=pltpu.CompilerParams(
            dimension_semantics=("arbitrary", "arbitrary", "arbitrary"), vmem_limit_bytes=VMEM_LIMIT),
        name="moba_branch",
    )(proj, proj, proj, proj)


MIX_TM = 512
MIX_TN = 512
OUT_TM = 512


def _mix_kernel(yn_ref, og_ref, ws_ref, wa_ref, gs_ref, ga_ref, bs_ref, ba_ref, o_ref):
    y_ssm = _dot(yn_ref[...], ws_ref[...])
    y_att = _dot(og_ref[...], wa_ref[...])
    g_ssm = _sigmoid(gs_ref[...].astype(F32) + bs_ref[...])
    g_att = _sigmoid(ga_ref[...].astype(F32) + ba_ref[...])
    o_ref[...] = (g_ssm * y_ssm + g_att * y_att).astype(o_ref.dtype)


def _gated_merge(yn, og, w_ssm, w_att, proj, gate_bias):
    t = yn.shape[0]
    nj = D_MODEL // MIX_TN
    return pl.pallas_call(
        _mix_kernel,
        grid=(t // MIX_TM, nj),
        in_specs=[
            pl.BlockSpec((MIX_TM, D_INNER), lambda i, j: (i, 0)),
            pl.BlockSpec((MIX_TM, D_MODEL), lambda i, j: (i, 0)),
            pl.BlockSpec((D_INNER, MIX_TN), lambda i, j: (0, j)),
            pl.BlockSpec((D_MODEL, MIX_TN), lambda i, j: (0, j)),
            pl.BlockSpec((MIX_TM, MIX_TN), lambda i, j: (i, OFF_GM // MIX_TN + j)),
            pl.BlockSpec((MIX_TM, MIX_TN), lambda i, j: (i, OFF_GM // MIX_TN + nj + j)),
            pl.BlockSpec((1, MIX_TN), lambda i, j: (0, j)),
            pl.BlockSpec((1, MIX_TN), lambda i, j: (0, nj + j)),
        ],
        out_specs=pl.BlockSpec((MIX_TM, MIX_TN), lambda i, j: (i, j)),
        out_shape=jax.ShapeDtypeStruct((t, D_MODEL), BF16),
        compiler_params=pltpu.CompilerParams(
            dimension_semantics=("arbitrary", "arbitrary"), vmem_limit_bytes=VMEM_LIMIT),
        name="gated_merge",
    )(yn, og, w_ssm, w_att, proj, proj, gate_bias, gate_bias)


def _out_kernel(m_ref, w_ref, x_ref, fw_ref, o_ref):
    r = x_ref[...] + _dot(m_ref[...], w_ref[...])
    ms = jnp.mean(r * r, axis=-1, keepdims=True)
    o_ref[...] = r * lax.rsqrt(ms + EPS) * fw_ref[...]


def _output(mixed, w_out, x2, final_w):
    t = x2.shape[0]
    return pl.pallas_call(
        _out_kernel,
        grid=(t // OUT_TM,),
        in_specs=[
            pl.BlockSpec((OUT_TM, D_MODEL), lambda i: (i, 0)),
            pl.BlockSpec((D_MODEL, D_MODEL), lambda i: (0, 0)),
            pl.BlockSpec((OUT_TM, D_MODEL), lambda i: (i, 0)),
            pl.BlockSpec((1, D_MODEL), lambda i: (0, 0)),
        ],
        out_specs=pl.BlockSpec((OUT_TM, D_MODEL), lambda i: (i, 0)),
        out_shape=jax.ShapeDtypeStruct((t, D_MODEL), F32),
        compiler_params=pltpu.CompilerParams(
            dimension_semantics=("arbitrary",), vmem_limit_bytes=VMEM_LIMIT),
        name="output_norm",
    )(mixed, w_out, x2, final_w)


def _pad_lanes(v):
    return jnp.pad(v, (0, LANES - v.shape[0])).reshape(1, LANES)


def kernel(x, norm_w, w_in, conv_w, conv_b, dt_bias, A_log, D_skip, ssm_norm_w,
           w_ssm_proj, w_attn_proj, gate_bias, w_out, final_norm_w):
    batch, seq, _ = x.shape
    depth = norm_w.shape[0]
    dt_lo = D_INNER + (D_INNER + 2 * N_GROUPS * N_STATE)
    dt_hi = dt_lo + N_HEADS
    head_of_channel = jnp.arange(D_INNER, dtype=jnp.int32) // HEAD_DIM
    expand = (jnp.arange(LANES, dtype=jnp.int32)[:, None] == head_of_channel[None, :]).astype(BF16)

    assert depth == 1, "the block is built for a single layer"
    l = 0
    x2 = x.reshape(batch * seq, D_MODEL)
    w = w_in[l]
    w_main = jnp.concatenate([w[:, :dt_lo], w[:, dt_hi:]], axis=1).astype(BF16)
    w_dt = jnp.pad(w[:, dt_lo:dt_hi], ((0, 0), (0, LANES - N_HEADS)))
    wdt_hi = w_dt.astype(BF16)
    wdt_lo = (w_dt - wdt_hi.astype(F32)).astype(BF16)

    proj, dt_raw = _in_projection(x2, norm_w[l].reshape(1, D_MODEL), w_main, wdt_hi, wdt_lo)

    yn = _ssd_branch(
        proj, dt_raw, conv_w[l], conv_b[l].reshape(1, -1),
        _pad_lanes(dt_bias[l]), _pad_lanes(A_log[l]),
        jnp.repeat(D_skip[l], HEAD_DIM).reshape(1, D_INNER),
        ssm_norm_w[l].reshape(1, D_INNER), expand, batch, seq)

    og = _moba_branch(proj, batch, seq)

    mixed = _gated_merge(yn, og, w_ssm_proj[l].astype(BF16), w_attn_proj[l].astype(BF16),
                         proj, gate_bias[l].reshape(1, -1))
    out = _output(mixed, w_out[l].astype(BF16), x2, final_norm_w.reshape(1, D_MODEL))
    return out.reshape(batch, seq, D_MODEL)
```

```python
import jax
import jax.numpy as jnp
from jax import lax
from jax.experimental import pallas as pl
from jax.experimental.pallas import tpu as pltpu

F32 = jnp.float32
BF16 = jnp.bfloat16

D_MODEL = 2048
D_INNER = 4096
HEAD_DIM = 64
N_HEADS = 64
N_GROUPS = 8
HEADS_PER_GROUP = N_HEADS // N_GROUPS
GROUP_W = D_INNER // N_GROUPS
N_STATE = 128
CONV_W = GROUP_W + 2 * N_STATE
CONV_K = 4
CHUNK = 256
HALF = CHUNK // 2
ATTN_HEADS = 16
ATTN_DH = 128
MOBA_BLOCK = 256
MOBA_TOPK = 3
EPS = 1e-6
NEG = -1e30

OFF_Z = 0
OFF_X = 4096
OFF_B = 8192
OFF_C = 9216
OFF_Q = 10240
OFF_K = 12288
OFF_V = 14336
OFF_GA = 16384
OFF_GM = 18432
PROJ_W = 22528

LANES = 128
SUBLANES = 8
BF16_ROWS = 16
VMEM_LIMIT = 56 * 1024 * 1024


def _dot(a, b):
    return jnp.dot(a, b, preferred_element_type=F32)


def _dot_nt(a, b):
    return lax.dot_general(a, b, (((1,), (1,)), ((), ())), preferred_element_type=F32)


def _dot_tn(a, b):
    return lax.dot_general(a, b, (((0,), (0,)), ((), ())), preferred_element_type=F32)


def _split2(a):
    hi = a.astype(BF16)
    lo = (a - hi.astype(F32)).astype(BF16)
    return hi, lo


def _split3(a):
    a1 = a.astype(BF16)
    r1 = a - a1.astype(F32)
    a2 = r1.astype(BF16)
    a3 = (r1 - a2.astype(F32)).astype(BF16)
    return a1, a2, a3


def _sigmoid(x):
    return 0.5 + 0.5 * jnp.tanh(0.5 * x)


def _silu(x):
    h = 0.5 * x
    return h + h * jnp.tanh(h)


IN_TM = 1024
IN_TN = 1024


def _inproj_kernel(x_ref, nw_ref, w_ref, wdh_ref, wdl_ref, o_ref, dt_ref, h_ref):
    j = pl.program_id(1)

    @pl.when(j == 0)
    def _():
        x = x_ref[...]
        ms = jnp.mean(x * x, axis=-1, keepdims=True)
        h = x * lax.rsqrt(ms + EPS) * nw_ref[...]
        hb, hl = _split2(h)
        h_ref[...] = hb
        wdh = wdh_ref[...]
        dt_ref[...] = _dot(hb, wdh) + _dot(hb, wdl_ref[...]) + _dot(hl, wdh)

    o_ref[...] = _dot(h_ref[...], w_ref[...]).astype(o_ref.dtype)


def _in_projection(x2, norm_w, w_main, wdt_hi, wdt_lo):
    t = x2.shape[0]
    grid = (t // IN_TM, PROJ_W // IN_TN)
    return pl.pallas_call(
        _inproj_kernel,
        grid=grid,
        in_specs=[
            pl.BlockSpec((IN_TM, D_MODEL), lambda i, j: (i, 0)),
            pl.BlockSpec((1, D_MODEL), lambda i, j: (0, 0)),
            pl.BlockSpec((D_MODEL, IN_TN), lambda i, j: (0, j)),
            pl.BlockSpec((D_MODEL, LANES), lambda i, j: (0, 0)),
            pl.BlockSpec((D_MODEL, LANES), lambda i, j: (0, 0)),
        ],
        out_specs=[
            pl.BlockSpec((IN_TM, IN_TN), lambda i, j: (i, j)),
            pl.BlockSpec((IN_TM, LANES), lambda i, j: (i, 0)),
        ],
        out_shape=[
            jax.ShapeDtypeStruct((t, PROJ_W), BF16),
            jax.ShapeDtypeStruct((t, LANES), F32),
        ],
        scratch_shapes=[pltpu.VMEM((IN_TM, D_MODEL), BF16)],
        compiler_params=pltpu.CompilerParams(
            dimension_semantics=("arbitrary", "arbitrary"), vmem_limit_bytes=VMEM_LIMIT),
        name="in_projection",
    )(x2, norm_w, w_main, wdt_hi, wdt_lo)


def _ssd_kernel(z_ref, x_ref, b_ref, c_ref, dtraw_ref,
                cwx_ref, cbx_ref, cwb_ref, cbb_ref, cwc_ref, cbc_ref,
                dtb_ref, alog_ref, dskip_ref, nw_ref, eg_ref, shift_ref,
                y_ref,
                state_ref, tail_ref, halo_ref, cumg_ref, cumt_ref, hq_ref, dlast_ref):
    c = pl.program_id(1)
    g = pl.program_id(2)

    row = lax.broadcasted_iota(jnp.int32, (CHUNK, CHUNK), 0)
    col = lax.broadcasted_iota(jnp.int32, (CHUNK, CHUNK), 1)
    causal = row >= col

    @pl.when(g == 0)
    def _():
        dtr = dtraw_ref[...] + dtb_ref[...]
        dt = jnp.maximum(dtr, 0.0) + jnp.log1p(jnp.exp(-jnp.abs(dtr)))
        a = dt * (-jnp.exp(alog_ref[...]))
        a1, a2, a3 = _split3(a)
        tril = causal.astype(BF16)
        cum = _dot(tril, a1) + _dot(tril, a2) + _dot(tril, a3)
        cumt_ref[...] = cum.T
        for gg in range(N_GROUPS):
            sh = (LANES - HEADS_PER_GROUP * gg) % LANES
            cumg_ref[gg] = cum if sh == 0 else pltpu.roll(cum, sh, axis=1)
        last = cum[CHUNK - 1:CHUNK, :]
        hq_ref[0] = dt.astype(BF16)
        hq_ref[1] = jnp.exp(cum).astype(BF16)
        hq_ref[2] = jnp.exp(last - cum).astype(BF16)
        dlast_ref[...] = jnp.broadcast_to(jnp.exp(last), (SUBLANES, LANES))

    @pl.when(c == 0)
    def _():
        tail_ref[g] = jnp.zeros((SUBLANES, CONV_W), F32)
        state_ref[g] = jnp.zeros((N_STATE, GROUP_W), F32)

    xin = jnp.concatenate([x_ref[...], b_ref[...], c_ref[...]], axis=1)
    xin_f = xin.astype(F32)
    cw = jnp.concatenate([cwx_ref[...], cwb_ref[...], cwc_ref[...]], axis=1)
    cbias = jnp.concatenate([cbx_ref[...], cbb_ref[...], cbc_ref[...]], axis=1)
    halo_ref[0:SUBLANES, :] = tail_ref[g]
    halo_ref[SUBLANES:2 * SUBLANES, :] = jnp.zeros((SUBLANES, CONV_W), F32)
    tail_ref[g] = xin_f[CHUNK - SUBLANES:CHUNK, :]
    acc = cbias + cw[CONV_K - 1:CONV_K, :] * xin_f
    cor = jnp.zeros((SUBLANES, CONV_W), F32)
    for d in range(1, CONV_K):
        wk = cw[CONV_K - 1 - d:CONV_K - d, :]
        acc = acc + wk * _dot(shift_ref[d - 1], xin)
        cor = cor + wk * halo_ref[SUBLANES - d:2 * SUBLANES - d, :]
    acc = jnp.concatenate([acc[0:SUBLANES] + cor, acc[SUBLANES:]], axis=0)
    act = _silu(acc)
    xc = act[:, 0:GROUP_W]
    bmat = act[:, GROUP_W:GROUP_W + N_STATE].astype(BF16)
    cmat = act[:, GROUP_W + N_STATE:CONV_W].astype(BF16)

    eg = eg_ref[0]
    e_dt = _dot(hq_ref[0], eg)
    e_cum = _dot(hq_ref[1], eg)
    e_end = _dot(hq_ref[2], eg)
    dh, dlo = _split2(dlast_ref[...])
    dl = _dot(dh, eg) + _dot(dlo, eg)

    cb = _dot_nt(cmat, bmat)
    cb_top = cb[0:HALF, 0:HALF]
    cb_bot = cb[HALF:CHUNK, :]
    causal_top = causal[0:HALF, 0:HALF]
    causal_bot = causal[HALF:CHUNK, :]
    xdt = xc * e_dt
    xdt_b = xdt.astype(BF16)
    xde_b = (xdt * e_end).astype(BF16)

    cg = cumg_ref[g]
    lane = lax.broadcasted_iota(jnp.int32, (CHUNK, LANES), 1)
    ys = []
    for pair in range(HEADS_PER_GROUP // 2):
        xp = xdt_b[:, LANES * pair:LANES * (pair + 1)]
        top = None
        bot = None
        for sub in range(2):
            hh = 2 * pair + sub
            colv = cg[:, hh:hh + 1]
            rowv = cumt_ref[pl.ds(g * HEADS_PER_GROUP + hh, 1), :]
            seg_t = jnp.where(causal_top, colv[0:HALF] - rowv[:, 0:HALF], NEG)
            seg_b = jnp.where(causal_bot, colv[HALF:CHUNK] - rowv, NEG)
            w_t = (cb_top * jnp.exp(seg_t)).astype(BF16)
            w_b = (cb_bot * jnp.exp(seg_b)).astype(BF16)
            in_head = (lane >= HEAD_DIM * sub) & (lane < HEAD_DIM * (sub + 1))
            rhs = jnp.where(in_head, xp, jnp.zeros_like(xp))
            t_t = _dot(w_t, rhs[0:HALF])
            t_b = _dot(w_b, rhs)
            top = t_t if top is None else top + t_t
            bot = t_b if bot is None else bot + t_b
        ys.append(jnp.concatenate([top, bot], axis=0))
    y = jnp.concatenate(ys, axis=1)

    st = state_ref[g]
    y = y + _dot(cmat, st.astype(BF16)) * e_cum
    y = y + dskip_ref[...] * xc
    y = y * _silu(z_ref[...].astype(F32))
    ms = jnp.mean(y * y, axis=-1, keepdims=True)
    y_ref[...] = (y * lax.rsqrt(ms + EPS) * nw_ref[...]).astype(y_ref.dtype)

    state_ref[g] = st * dl[0:1, :] + _dot_tn(bmat, xde_b)


def _ssd_branch(proj, dt_raw, conv_w, conv_b, dtb_pad, alog_pad, dskip_exp, ssm_norm_w, expand, shifts,
                batch, seq):
    t = proj.shape[0]
    nc = seq // CHUNK

    def rows(b, c, g):
        return b * nc + c

    in_specs = [
        pl.BlockSpec((CHUNK, GROUP_W), lambda b, c, g: (rows(b, c, g), OFF_Z // GROUP_W + g)),
        pl.BlockSpec((CHUNK, GROUP_W), lambda b, c, g: (rows(b, c, g), OFF_X // GROUP_W + g)),
        pl.BlockSpec((CHUNK, N_STATE), lambda b, c, g: (rows(b, c, g), OFF_B // N_STATE + g)),
        pl.BlockSpec((CHUNK, N_STATE), lambda b, c, g: (rows(b, c, g), OFF_C // N_STATE + g)),
        pl.BlockSpec((CHUNK, LANES), lambda b, c, g: (rows(b, c, g), 0)),
        pl.BlockSpec((CONV_K, GROUP_W), lambda b, c, g: (0, g)),
        pl.BlockSpec((1, GROUP_W), lambda b, c, g: (0, g)),
        pl.BlockSpec((CONV_K, N_STATE), lambda b, c, g: (0, D_INNER // N_STATE + g)),
        pl.BlockSpec((1, N_STATE), lambda b, c, g: (0, D_INNER // N_STATE + g)),
        pl.BlockSpec((CONV_K, N_STATE), lambda b, c, g: (0, D_INNER // N_STATE + N_GROUPS + g)),
        pl.BlockSpec((1, N_STATE), lambda b, c, g: (0, D_INNER // N_STATE + N_GROUPS + g)),
        pl.BlockSpec((1, LANES), lambda b, c, g: (0, 0)),
        pl.BlockSpec((1, LANES), lambda b, c, g: (0, 0)),
        pl.BlockSpec((1, GROUP_W), lambda b, c, g: (0, g)),
        pl.BlockSpec((1, GROUP_W), lambda b, c, g: (0, g)),
        pl.BlockSpec((1, LANES, GROUP_W), lambda b, c, g: (g, 0, 0)),
        pl.BlockSpec((CONV_K - 1, CHUNK, CHUNK), lambda b, c, g: (0, 0, 0)),
    ]
    return pl.pallas_call(
        _ssd_kernel,
        grid=(batch, nc, N_GROUPS),
        in_specs=in_specs,
        out_specs=pl.BlockSpec((CHUNK, GROUP_W), lambda b, c, g: (rows(b, c, g), g)),
        out_shape=jax.ShapeDtypeStruct((t, D_INNER), BF16),
        scratch_shapes=[
            pltpu.VMEM((N_GROUPS, N_STATE, GROUP_W), F32),
            pltpu.VMEM((N_GROUPS, SUBLANES, CONV_W), F32),
            pltpu.VMEM((2 * SUBLANES, CONV_W), F32),
            pltpu.VMEM((N_GROUPS, CHUNK, LANES), F32),
            pltpu.VMEM((LANES, CHUNK), F32),
            pltpu.VMEM((3, CHUNK, LANES), BF16),
            pltpu.VMEM((SUBLANES, LANES), F32),
        ],
        compiler_params=pltpu.CompilerParams(
            dimension_semantics=("arbitrary", "arbitrary", "arbitrary"), vmem_limit_bytes=VMEM_LIMIT),
        name="ssd_branch",
    )(proj, proj, proj, proj, dt_raw, conv_w, conv_b, conv_w, conv_b, conv_w, conv_b,
      dtb_pad, alog_pad, dskip_exp, ssm_norm_w, expand, shifts)


HEADS_PER_STEP = 4
STEP_W = HEADS_PER_STEP * ATTN_DH
KV_PER_ITER = 2
ACC_ROWS = ATTN_DH + BF16_ROWS
EXP2_SCALE = (ATTN_DH ** -0.5) * 1.4426950408889634


def _moba_kernel(q_ref, k_ref, v_ref, g_ref, o_ref,
                 vt_ref, kmean_ref, bias_ref, sa_ref, mxa_ref, sb_ref, mxb_ref):
    i = pl.program_id(2)
    seq = k_ref.shape[0]
    nb = seq // MOBA_BLOCK
    n_groups = nb // KV_PER_ITER

    @pl.when(i == 0)
    def _():
        def prep(jb, carry):
            off = pl.multiple_of(jb * MOBA_BLOCK, MOBA_BLOCK)
            vt = v_ref[pl.ds(off, MOBA_BLOCK), :].astype(F32).T.astype(BF16)
            for a in range(HEADS_PER_STEP):
                vt_ref[jb, a, 0:ATTN_DH, :] = vt[ATTN_DH * a:ATTN_DH * (a + 1), :]
                vt_ref[jb, a, ATTN_DH:ACC_ROWS, :] = jnp.ones((BF16_ROWS, MOBA_BLOCK), BF16)
            kb = k_ref[pl.ds(off, MOBA_BLOCK), :].astype(F32)
            kmean_ref[pl.ds(jb, 1), :] = jnp.mean(kb, axis=0, keepdims=True)
            return carry
        lax.fori_loop(0, nb, prep, 0)

    km_hi, km_lo = _split2(kmean_ref[...])
    blk = lax.broadcasted_iota(jnp.int32, (nb, MOBA_BLOCK), 0)
    past = blk < i
    key = lax.broadcasted_iota(jnp.int32, (MOBA_BLOCK, MOBA_BLOCK), 0)
    qry = lax.broadcasted_iota(jnp.int32, (MOBA_BLOCK, MOBA_BLOCK), 1)
    own = pl.multiple_of(i * MOBA_BLOCK, MOBA_BLOCK)

    def scores(off, a, qa):
        kk = k_ref[pl.ds(off, MOBA_BLOCK), ATTN_DH * a:ATTN_DH * (a + 1)]
        return _dot_nt(jnp.concatenate([kk, kk], axis=1), qa)

    heads = range(HEADS_PER_STEP)
    qs = [q_ref[:, ATTN_DH * a:ATTN_DH * (a + 1)] for a in heads]
    qas = []
    for a in heads:
        q_hi, q_lo = _split2(qs[a].astype(F32) * EXP2_SCALE)
        qas.append(jnp.concatenate([q_hi, q_lo], axis=1))

    gates = []
    for a in heads:
        lo, hi = ATTN_DH * a, ATTN_DH * (a + 1)
        gate = _dot_nt(km_hi[:, lo:hi], qs[a]) + _dot_nt(km_lo[:, lo:hi], qs[a])
        gates.append(jnp.where(past, gate, NEG))
    sels = [jnp.zeros((nb, MOBA_BLOCK), jnp.bool_) for a in heads]
    for _ in range(MOBA_TOPK):
        for a in heads:
            mx = jnp.max(gates[a], axis=0, keepdims=True)
            idx = jnp.min(jnp.where(gates[a] == mx, blk, 4 * nb), axis=0, keepdims=True)
            hit = blk == idx
            sels[a] = sels[a] | (hit & past)
            gates[a] = jnp.where(hit, -jnp.inf, gates[a])
    for a in heads:
        bias_ref[a] = jnp.where(sels[a], 0.0, NEG)

    carry = []
    for a in heads:
        s = jnp.where(key <= qry, scores(own, a, qas[a]), NEG)
        m = jnp.max(s, axis=0, keepdims=True)
        p = jnp.exp2(s - m)
        carry += [m, _dot(vt_ref[i, a], p.astype(BF16))]

    def score_tile(a, u, grp, s_ref, mx_ref):
        j = KV_PER_ITER * grp + u
        s = scores(pl.multiple_of(j * MOBA_BLOCK, MOBA_BLOCK), a, qas[a])
        s_ref[a, u] = s
        r = a * KV_PER_ITER + u
        mx_ref[r:r + 1, :] = jnp.max(s, axis=0, keepdims=True) + bias_ref[a, pl.ds(j, 1), :]

    def score_stage(a, grp, s_ref, mx_ref):
        for u in range(KV_PER_ITER):
            score_tile(a, u, grp, s_ref, mx_ref)

    def half_step(grp, cur, nxt, carry):
        s_ref, mx_ref = cur
        nxt_grp = jnp.minimum(grp + 1, n_groups - 1)
        out = []
        for a in range(HEADS_PER_STEP):
            m, acc = carry[2 * a:2 * a + 2]
            mx = jnp.max(mx_ref[a * KV_PER_ITER:(a + 1) * KV_PER_ITER, :], axis=0, keepdims=True)
            m_new = jnp.maximum(m, mx)
            acc = jnp.exp2(m - m_new) * acc
            for u in range(KV_PER_ITER):
                j = KV_PER_ITER * grp + u
                m_eff = m_new - bias_ref[a, pl.ds(j, 1), :]
                p = jnp.exp2(s_ref[a, u] - m_eff)
                acc = acc + _dot(vt_ref[j, a], p.astype(BF16))
            score_stage(a, nxt_grp, *nxt)
            out += [m_new, acc]
        return out

    buf_a = (sa_ref, mxa_ref)
    buf_b = (sb_ref, mxb_ref)

    def body(tt, carry):
        carry = half_step(2 * tt, buf_a, buf_b, list(carry))
        carry = half_step(2 * tt + 1, buf_b, buf_a, carry)
        return tuple(carry)

    n_iter = (i + KV_PER_ITER - 1) // KV_PER_ITER
    for a in range(HEADS_PER_STEP):
        score_stage(a, 0, *buf_a)
    carry = lax.fori_loop(0, (n_iter + 1) // 2, body, tuple(carry))

    for a in range(HEADS_PER_STEP):
        lo, hi = ATTN_DH * a, ATTN_DH * (a + 1)
        acc = carry[2 * a + 1]
        denom = acc[ATTN_DH:ATTN_DH + 1, :]
        o = (acc[0:ATTN_DH, :] * (1.0 / denom)).T
        o_ref[:, lo:hi] = (o * _silu(g_ref[:, lo:hi].astype(F32))).astype(o_ref.dtype)


def _moba_branch(proj, batch, seq):
    t = proj.shape[0]
    nb = seq // MOBA_BLOCK
    assert nb % (2 * KV_PER_ITER) == 0

    def qrow(b, h, i):
        return b * nb + i

    score_buf = pltpu.VMEM((HEADS_PER_STEP, KV_PER_ITER, MOBA_BLOCK, MOBA_BLOCK), F32)
    max_buf = pltpu.VMEM((HEADS_PER_STEP * KV_PER_ITER, MOBA_BLOCK), F32)
    return pl.pallas_call(
        _moba_kernel,
        grid=(batch, ATTN_HEADS // HEADS_PER_STEP, nb),
        in_specs=[
            pl.BlockSpec((MOBA_BLOCK, STEP_W), lambda b, h, i: (qrow(b, h, i), OFF_Q // STEP_W + h)),
            pl.BlockSpec((seq, STEP_W), lambda b, h, i: (b, OFF_K // STEP_W + h)),
            pl.BlockSpec((seq, STEP_W), lambda b, h, i: (b, OFF_V // STEP_W + h)),
            pl.BlockSpec((MOBA_BLOCK, STEP_W), lambda b, h, i: (qrow(b, h, i), OFF_GA // STEP_W + h)),
        ],
        out_specs=pl.BlockSpec((MOBA_BLOCK, STEP_W), lambda b, h, i: (qrow(b, h, i), h)),
        out_shape=jax.ShapeDtypeStruct((t, ATTN_HEADS * ATTN_DH), BF16),
        scratch_shapes=[
            pltpu.VMEM((nb, HEADS_PER_STEP, ACC_ROWS, MOBA_BLOCK), BF16),
            pltpu.VMEM((nb, STEP_W), F32),
            pltpu.VMEM((HEADS_PER_STEP, nb, MOBA_BLOCK), F32),
            score_buf, max_buf, score_buf, max_buf,
        ],
        compiler_params=pltpu.CompilerParams(
            dimension_semantics=("arbitrary", "arbitrary", "arbitrary"), vmem_limit_bytes=VMEM_LIMIT),
        name="moba_branch",
    )(proj, proj, proj, proj)


MIX_TM = 512
MIX_TN = 512
OUT_TM = 512


def _mix_kernel(yn_ref, og_ref, ws_ref, wa_ref, gs_ref, ga_ref, bs_ref, ba_ref, o_ref):
    y_ssm = _dot(yn_ref[...], ws_ref[...])
    y_att = _dot(og_ref[...], wa_ref[...])
    g_ssm = _sigmoid(gs_ref[...].astype(F32) + bs_ref[...])
    g_att = _sigmoid(ga_ref[...].astype(F32) + ba_ref[...])
    o_ref[...] = (g_ssm * y_ssm + g_att * y_att).astype(o_ref.dtype)


def _gated_merge(yn, og, w_ssm, w_att, proj, gate_bias):
    t = yn.shape[0]
    nj = D_MODEL // MIX_TN
    return pl.pallas_call(
        _mix_kernel,
        grid=(t // MIX_TM, nj),
        in_specs=[
            pl.BlockSpec((MIX_TM, D_INNER), lambda i, j: (i, 0)),
            pl.BlockSpec((MIX_TM, D_MODEL), lambda i, j: (i, 0)),
            pl.BlockSpec((D_INNER, MIX_TN), lambda i, j: (0, j)),
            pl.BlockSpec((D_MODEL, MIX_TN), lambda i, j: (0, j)),
            pl.BlockSpec((MIX_TM, MIX_TN), lambda i, j: (i, OFF_GM // MIX_TN + j)),
            pl.BlockSpec((MIX_TM, MIX_TN), lambda i, j: (i, OFF_GM // MIX_TN + nj + j)),
            pl.BlockSpec((1, MIX_TN), lambda i, j: (0, j)),
            pl.BlockSpec((1, MIX_TN), lambda i, j: (0, nj + j)),
        ],
        out_specs=pl.BlockSpec((MIX_TM, MIX_TN), lambda i, j: (i, j)),
        out_shape=jax.ShapeDtypeStruct((t, D_MODEL), BF16),
        compiler_params=pltpu.CompilerParams(
            dimension_semantics=("arbitrary", "arbitrary"), vmem_limit_bytes=VMEM_LIMIT),
        name="gated_merge",
    )(yn, og, w_ssm, w_att, proj, proj, gate_bias, gate_bias)


def _out_kernel(m_ref, w_ref, x_ref, fw_ref, o_ref):
    r = x_ref[...] + _dot(m_ref[...], w_ref[...])
    ms = jnp.mean(r * r, axis=-1, keepdims=True)
    o_ref[...] = r * lax.rsqrt(ms + EPS) * fw_ref[...]


def _output(mixed, w_out, x2, final_w):
    t = x2.shape[0]
    return pl.pallas_call(
        _out_kernel,
        grid=(t // OUT_TM,),
        in_specs=[
            pl.BlockSpec((OUT_TM, D_MODEL), lambda i: (i, 0)),
            pl.BlockSpec((D_MODEL, D_MODEL), lambda i: (0, 0)),
            pl.BlockSpec((OUT_TM, D_MODEL), lambda i: (i, 0)),
            pl.BlockSpec((1, D_MODEL), lambda i: (0, 0)),
        ],
        out_specs=pl.BlockSpec((OUT_TM, D_MODEL), lambda i: (i, 0)),
        out_shape=jax.ShapeDtypeStruct((t, D_MODEL), F32),
        compiler_params=pltpu.CompilerParams(
            dimension_semantics=("arbitrary",), vmem_limit_bytes=VMEM_LIMIT),
        name="output_norm",
    )(mixed, w_out, x2, final_w)


def _pad_lanes(v):
    return jnp.pad(v, (0, LANES - v.shape[0])).reshape(1, LANES)


def kernel(x, norm_w, w_in, conv_w, conv_b, dt_bias, A_log, D_skip, ssm_norm_w,
           w_ssm_proj, w_attn_proj, gate_bias, w_out, final_norm_w):
    batch, seq, _ = x.shape
    depth = norm_w.shape[0]
    assert depth == 1, "the block is built for a single layer"
    l = 0
    dt_lo = D_INNER + (D_INNER + 2 * N_GROUPS * N_STATE)
    dt_hi = dt_lo + N_HEADS

    head_of_channel = jnp.arange(D_INNER, dtype=jnp.int32) // HEAD_DIM
    expand = (jnp.arange(LANES, dtype=jnp.int32)[:, None] == head_of_channel[None, :]).astype(BF16)
    expand = expand.reshape(LANES, N_GROUPS, GROUP_W).transpose(1, 0, 2)
    rows = jnp.arange(CHUNK, dtype=jnp.int32)
    shifts = jnp.stack([(rows[:, None] - rows[None, :] == d).astype(BF16) for d in range(1, CONV_K)])

    x2 = x.reshape(batch * seq, D_MODEL)
    w = w_in[l]
    w_main = jnp.concatenate([w[:, :dt_lo], w[:, dt_hi:]], axis=1).astype(BF16)
    w_dt = jnp.pad(w[:, dt_lo:dt_hi], ((0, 0), (0, LANES - N_HEADS)))
    wdt_hi, wdt_lo = _split2(w_dt)

    proj, dt_raw = _in_projection(x2, norm_w[l].reshape(1, D_MODEL), w_main, wdt_hi, wdt_lo)

    yn = _ssd_branch(
        proj, dt_raw, conv_w[l], conv_b[l].reshape(1, -1),
        _pad_lanes(dt_bias[l]), _pad_lanes(A_log[l]),
        jnp.repeat(D_skip[l], HEAD_DIM).reshape(1, D_INNER),
        ssm_norm_w[l].reshape(1, D_INNER), expand, shifts, batch, seq)

    og = _moba_branch(proj, batch, seq)

    mixed = _gated_merge(yn, og, w_ssm_proj[l].astype(BF16), w_attn_proj[l].astype(BF16),
                         proj, gate_bias[l].reshape(1, -1))
    out = _output(mixed, w_out[l].astype(BF16), x2, final_norm_w.reshape(1, D_MODEL))
    return out.reshape(batch, seq, D_MODEL)
```

```python
import jax
import jax.numpy as jnp
from jax import lax
from jax.experimental import pallas as pl
from jax.experimental.pallas import tpu as pltpu

F32 = jnp.float32
BF16 = jnp.bfloat16

D_MODEL = 2048
D_INNER = 4096
HEAD_DIM = 64
N_HEADS = 64
N_GROUPS = 8
HEADS_PER_GROUP = N_HEADS // N_GROUPS
GROUP_W = D_INNER // N_GROUPS
N_STATE = 128
CONV_W = GROUP_W + 2 * N_STATE
CONV_K = 4
CHUNK = 256
HALF = CHUNK // 2
ATTN_HEADS = 16
ATTN_DH = 128
MOBA_BLOCK = 256
MOBA_TOPK = 3
EPS = 1e-6
NEG = -1e30

OFF_Z = 0
OFF_X = 4096
OFF_B = 8192
OFF_C = 9216
OFF_Q = 10240
OFF_K = 12288
OFF_V = 14336
OFF_GA = 16384
OFF_GM = 18432
PROJ_W = 22528

LANES = 128
SUBLANES = 8
BF16_ROWS = 16
VMEM_LIMIT = 56 * 1024 * 1024


def _dot(a, b):
    return jnp.dot(a, b, preferred_element_type=F32)


def _dot_nt(a, b):
    return lax.dot_general(a, b, (((1,), (1,)), ((), ())), preferred_element_type=F32)


def _dot_tn(a, b):
    return lax.dot_general(a, b, (((0,), (0,)), ((), ())), preferred_element_type=F32)


def _split2(a):
    hi = a.astype(BF16)
    lo = (a - hi.astype(F32)).astype(BF16)
    return hi, lo


def _split3(a):
    a1 = a.astype(BF16)
    r1 = a - a1.astype(F32)
    a2 = r1.astype(BF16)
    a3 = (r1 - a2.astype(F32)).astype(BF16)
    return a1, a2, a3


def _sigmoid(x):
    return 0.5 + 0.5 * jnp.tanh(0.5 * x)


def _silu(x):
    h = 0.5 * x
    return h + h * jnp.tanh(h)


IN_TM = 1024
IN_TN = 1024


def _inproj_kernel(x_ref, nw_ref, w_ref, wdh_ref, wdl_ref, o_ref, dt_ref, h_ref):
    j = pl.program_id(1)

    @pl.when(j == 0)
    def _():
        x = x_ref[...]
        ms = jnp.mean(x * x, axis=-1, keepdims=True)
        h = x * lax.rsqrt(ms + EPS) * nw_ref[...]
        hb, hl = _split2(h)
        h_ref[...] = hb
        wdh = wdh_ref[...]
        dt_ref[...] = _dot(hb, wdh) + _dot(hb, wdl_ref[...]) + _dot(hl, wdh)

    o_ref[...] = _dot(h_ref[...], w_ref[...]).astype(o_ref.dtype)


def _in_projection(x2, norm_w, w_main, wdt_hi, wdt_lo):
    t = x2.shape[0]
    grid = (t // IN_TM, PROJ_W // IN_TN)
    return pl.pallas_call(
        _inproj_kernel,
        grid=grid,
        in_specs=[
            pl.BlockSpec((IN_TM, D_MODEL), lambda i, j: (i, 0)),
            pl.BlockSpec((1, D_MODEL), lambda i, j: (0, 0)),
            pl.BlockSpec((D_MODEL, IN_TN), lambda i, j: (0, j)),
            pl.BlockSpec((D_MODEL, LANES), lambda i, j: (0, 0)),
            pl.BlockSpec((D_MODEL, LANES), lambda i, j: (0, 0)),
        ],
        out_specs=[
            pl.BlockSpec((IN_TM, IN_TN), lambda i, j: (i, j)),
            pl.BlockSpec((IN_TM, LANES), lambda i, j: (i, 0)),
        ],
        out_shape=[
            jax.ShapeDtypeStruct((t, PROJ_W), BF16),
            jax.ShapeDtypeStruct((t, LANES), F32),
        ],
        scratch_shapes=[pltpu.VMEM((IN_TM, D_MODEL), BF16)],
        compiler_params=pltpu.CompilerParams(
            dimension_semantics=("arbitrary", "arbitrary"), vmem_limit_bytes=VMEM_LIMIT),
        name="in_projection",
    )(x2, norm_w, w_main, wdt_hi, wdt_lo)


def _ssd_kernel(z_ref, x_ref, bc_ref, dtraw_ref, cw_ref, cbias_ref,
                dtb_ref, alog_ref, dskip_ref, nw_ref, e_ref, shift_ref,
                y_ref,
                state_ref, tail_ref, halo_ref, cumg_ref, cumt_ref, hq_ref, dlast_ref):
    c = pl.program_id(1)

    row = lax.broadcasted_iota(jnp.int32, (CHUNK, CHUNK), 0)
    col = lax.broadcasted_iota(jnp.int32, (CHUNK, CHUNK), 1)
    causal = row >= col

    def chunk_quantities():
        dtr = dtraw_ref[...] + dtb_ref[...]
        dt = jnp.maximum(dtr, 0.0) + jnp.log1p(jnp.exp(-jnp.abs(dtr)))
        a = dt * (-jnp.exp(alog_ref[...]))
        a1, a2, a3 = _split3(a)
        tril = causal.astype(BF16)
        cum = _dot(tril, a1) + _dot(tril, a2) + _dot(tril, a3)
        cumt_ref[...] = cum.T
        for gg in range(N_GROUPS):
            sh = (LANES - HEADS_PER_GROUP * gg) % LANES
            cumg_ref[gg] = cum if sh == 0 else pltpu.roll(cum, sh, axis=1)
        last = cum[CHUNK - 1:CHUNK, :]
        hq_ref[0] = dt.astype(BF16)
        hq_ref[1] = jnp.exp(cum).astype(BF16)
        hq_ref[2] = jnp.exp(last - cum).astype(BF16)
        dlast_ref[...] = jnp.broadcast_to(jnp.exp(last), (SUBLANES, LANES))

    chunk_quantities()

    @pl.when(c == 0)
    def _():
        tail_ref[...] = jnp.zeros(tail_ref.shape, F32)
        state_ref[...] = jnp.zeros(state_ref.shape, F32)

    causal_top = causal[0:HALF, 0:HALF]
    causal_bot = causal[HALF:CHUNK, :]
    lane = lax.broadcasted_iota(jnp.int32, (CHUNK, LANES), 1)

    def group(g, carry):
        gx = pl.multiple_of(g * GROUP_W, GROUP_W)
        gb = pl.multiple_of(g * N_STATE, N_STATE)
        gc = pl.multiple_of(N_GROUPS * N_STATE + g * N_STATE, N_STATE)

        def conv_channels(ref):
            return jnp.concatenate([ref[:, pl.ds(gx, GROUP_W)],
                                    ref[:, pl.ds(pl.multiple_of(D_INNER + gb, N_STATE), N_STATE)],
                                    ref[:, pl.ds(pl.multiple_of(D_INNER + gc, N_STATE), N_STATE)]], axis=1)

        xin = jnp.concatenate([x_ref[:, pl.ds(gx, GROUP_W)], bc_ref[:, pl.ds(gb, N_STATE)],
                               bc_ref[:, pl.ds(gc, N_STATE)]], axis=1)
        xin_f = xin.astype(F32)
        cw = conv_channels(cw_ref)
        cbias = conv_channels(cbias_ref)
        halo_ref[0:SUBLANES, :] = tail_ref[g]
        halo_ref[SUBLANES:2 * SUBLANES, :] = jnp.zeros((SUBLANES, CONV_W), F32)
        tail_ref[g] = xin_f[CHUNK - SUBLANES:CHUNK, :]
        acc = cbias + cw[CONV_K - 1:CONV_K, :] * xin_f
        cor = jnp.zeros((SUBLANES, CONV_W), F32)
        for d in range(1, CONV_K):
            wk = cw[CONV_K - 1 - d:CONV_K - d, :]
            acc = acc + wk * _dot(shift_ref[d - 1], xin)
            cor = cor + wk * halo_ref[SUBLANES - d:2 * SUBLANES - d, :]
        acc = jnp.concatenate([acc[0:SUBLANES] + cor, acc[SUBLANES:]], axis=0)
        act = _silu(acc)
        xc = act[:, 0:GROUP_W]
        bmat = act[:, GROUP_W:GROUP_W + N_STATE].astype(BF16)
        cmat = act[:, GROUP_W + N_STATE:CONV_W].astype(BF16)

        eg = e_ref[g]
        e_dt = _dot(hq_ref[0], eg)
        e_cum = _dot(hq_ref[1], eg)
        e_end = _dot(hq_ref[2], eg)
        dh, dlo = _split2(dlast_ref[...])
        dl = _dot(dh, eg) + _dot(dlo, eg)

        cb = _dot_nt(cmat, bmat)
        cb_top = cb[0:HALF, 0:HALF]
        cb_bot = cb[HALF:CHUNK, :]
        xdt = xc * e_dt
        xdt_b = xdt.astype(BF16)
        xde_b = (xdt * e_end).astype(BF16)

        cg = cumg_ref[g]
        ys = []
        for pair in range(HEADS_PER_GROUP // 2):
            xp = xdt_b[:, LANES * pair:LANES * (pair + 1)]
            top = None
            bot = None
            for sub in range(2):
                hh = 2 * pair + sub
                colv = cg[:, hh:hh + 1]
                rowv = cumt_ref[pl.ds(g * HEADS_PER_GROUP + hh, 1), :]
                seg_t = jnp.where(causal_top, colv[0:HALF] - rowv[:, 0:HALF], NEG)
                seg_b = jnp.where(causal_bot, colv[HALF:CHUNK] - rowv, NEG)
                w_t = (cb_top * jnp.exp(seg_t)).astype(BF16)
                w_b = (cb_bot * jnp.exp(seg_b)).astype(BF16)
                in_head = (lane >= HEAD_DIM * sub) & (lane < HEAD_DIM * (sub + 1))
                rhs = jnp.where(in_head, xp, jnp.zeros_like(xp))
                t_t = _dot(w_t, rhs[0:HALF])
                t_b = _dot(w_b, rhs)
                top = t_t if top is None else top + t_t
                bot = t_b if bot is None else bot + t_b
            ys.append(jnp.concatenate([top, bot], axis=0))
        y = jnp.concatenate(ys, axis=1)

        st = state_ref[g]
        y = y + _dot(cmat, st.astype(BF16)) * e_cum
        y = y + dskip_ref[:, pl.ds(gx, GROUP_W)] * xc
        y = y * _silu(z_ref[:, pl.ds(gx, GROUP_W)].astype(F32))
        ms = jnp.mean(y * y, axis=-1, keepdims=True)
        y = y * lax.rsqrt(ms + EPS) * nw_ref[:, pl.ds(gx, GROUP_W)]
        y_ref[:, pl.ds(gx, GROUP_W)] = y.astype(y_ref.dtype)

        state_ref[g] = st * dl[0:1, :] + _dot_tn(bmat, xde_b)
        return carry

    lax.fori_loop(0, N_GROUPS, group, 0)


def _ssd_branch(proj, dt_raw, conv_w, conv_b, dtb_pad, alog_pad, dskip_exp, ssm_norm_w, expand, shifts,
                batch, seq):
    t = proj.shape[0]
    nc = seq // CHUNK

    bc_w = 2 * N_GROUPS * N_STATE
    conv_ch = D_INNER + bc_w

    def rows(b, c):
        return b * nc + c

    in_specs = [
        pl.BlockSpec((CHUNK, D_INNER), lambda b, c: (rows(b, c), OFF_Z // D_INNER)),
        pl.BlockSpec((CHUNK, D_INNER), lambda b, c: (rows(b, c), OFF_X // D_INNER)),
        pl.BlockSpec((CHUNK, bc_w), lambda b, c: (rows(b, c), OFF_B // bc_w)),
        pl.BlockSpec((CHUNK, LANES), lambda b, c: (rows(b, c), 0)),
        pl.BlockSpec((CONV_K, conv_ch), lambda b, c: (0, 0)),
        pl.BlockSpec((1, conv_ch), lambda b, c: (0, 0)),
        pl.BlockSpec((1, LANES), lambda b, c: (0, 0)),
        pl.BlockSpec((1, LANES), lambda b, c: (0, 0)),
        pl.BlockSpec((1, D_INNER), lambda b, c: (0, 0)),
        pl.BlockSpec((1, D_INNER), lambda b, c: (0, 0)),
        pl.BlockSpec((N_GROUPS, LANES, GROUP_W), lambda b, c: (0, 0, 0)),
        pl.BlockSpec((CONV_K - 1, CHUNK, CHUNK), lambda b, c: (0, 0, 0)),
    ]
    return pl.pallas_call(
        _ssd_kernel,
        grid=(batch, nc),
        in_specs=in_specs,
        out_specs=pl.BlockSpec((CHUNK, D_INNER), lambda b, c: (rows(b, c), 0)),
        out_shape=jax.ShapeDtypeStruct((t, D_INNER), BF16),
        scratch_shapes=[
            pltpu.VMEM((N_GROUPS, N_STATE, GROUP_W), F32),
            pltpu.VMEM((N_GROUPS, SUBLANES, CONV_W), F32),
            pltpu.VMEM((2 * SUBLANES, CONV_W), F32),
            pltpu.VMEM((N_GROUPS, CHUNK, LANES), F32),
            pltpu.VMEM((LANES, CHUNK), F32),
            pltpu.VMEM((3, CHUNK, LANES), BF16),
            pltpu.VMEM((SUBLANES, LANES), F32),
        ],
        compiler_params=pltpu.CompilerParams(
            dimension_semantics=("arbitrary", "arbitrary"), vmem_limit_bytes=VMEM_LIMIT),
        name="ssd_branch",
    )(proj, proj, proj, dt_raw, conv_w, conv_b, dtb_pad, alog_pad, dskip_exp, ssm_norm_w, expand, shifts)


HEADS_PER_STEP = 4
STEP_W = HEADS_PER_STEP * ATTN_DH
KV_PER_ITER = 2
ACC_ROWS = ATTN_DH + BF16_ROWS
EXP2_SCALE = (ATTN_DH ** -0.5) * 1.4426950408889634


def _moba_kernel(q_ref, k_ref, v_ref, g_ref, o_ref,
                 vt_ref, kmean_ref, bias_ref, sa_ref, mxa_ref, sb_ref, mxb_ref, acc_ref):
    i = pl.program_id(2)
    seq = k_ref.shape[0]
    nb = seq // MOBA_BLOCK
    n_groups = nb // KV_PER_ITER

    @pl.when(i == 0)
    def _():
        def prep(jb, carry):
            off = pl.multiple_of(jb * MOBA_BLOCK, MOBA_BLOCK)
            vt = v_ref[pl.ds(off, MOBA_BLOCK), :].astype(F32).T.astype(BF16)
            for a in range(HEADS_PER_STEP):
                vt_ref[jb, a, 0:ATTN_DH, :] = vt[ATTN_DH * a:ATTN_DH * (a + 1), :]
                vt_ref[jb, a, ATTN_DH:ACC_ROWS, :] = jnp.ones((BF16_ROWS, MOBA_BLOCK), BF16)
            kb = k_ref[pl.ds(off, MOBA_BLOCK), :].astype(F32)
            kmean_ref[pl.ds(jb, 1), :] = jnp.mean(kb, axis=0, keepdims=True)
            return carry
        lax.fori_loop(0, nb, prep, 0)

    km_hi, km_lo = _split2(kmean_ref[...])
    blk = lax.broadcasted_iota(jnp.int32, (nb, MOBA_BLOCK), 0)
    past = blk < i
    key = lax.broadcasted_iota(jnp.int32, (MOBA_BLOCK, MOBA_BLOCK), 0)
    qry = lax.broadcasted_iota(jnp.int32, (MOBA_BLOCK, MOBA_BLOCK), 1)
    own = pl.multiple_of(i * MOBA_BLOCK, MOBA_BLOCK)

    def scores(off, a, qa):
        kk = k_ref[pl.ds(off, MOBA_BLOCK), ATTN_DH * a:ATTN_DH * (a + 1)]
        return _dot_nt(jnp.concatenate([kk, kk], axis=1), qa)

    heads = range(HEADS_PER_STEP)
    qs = [q_ref[:, ATTN_DH * a:ATTN_DH * (a + 1)] for a in heads]
    qas = []
    for a in heads:
        q_hi, q_lo = _split2(qs[a].astype(F32) * EXP2_SCALE)
        qas.append(jnp.concatenate([q_hi, q_lo], axis=1))

    gates = []
    for a in heads:
        lo, hi = ATTN_DH * a, ATTN_DH * (a + 1)
        gate = _dot_nt(km_hi[:, lo:hi], qs[a]) + _dot_nt(km_lo[:, lo:hi], qs[a])
        gates.append(jnp.where(past, gate, NEG))
    sels = [jnp.zeros((nb, MOBA_BLOCK), jnp.bool_) for a in heads]
    for _ in range(MOBA_TOPK):
        for a in heads:
            mx = jnp.max(gates[a], axis=0, keepdims=True)
            idx = jnp.min(jnp.where(gates[a] == mx, blk, 4 * nb), axis=0, keepdims=True)
            hit = blk == idx
            sels[a] = sels[a] | (hit & past)
            gates[a] = jnp.where(hit, -jnp.inf, gates[a])
    for a in heads:
        bias_ref[a] = jnp.where(sels[a], 0.0, NEG)

    carry = []
    for a in heads:
        s = jnp.where(key <= qry, scores(own, a, qas[a]), NEG)
        m = jnp.max(s, axis=0, keepdims=True)
        p = jnp.exp2(s - m)
        acc_ref[a] = _dot(vt_ref[i, a], p.astype(BF16))
        carry.append(m)

    def score_tile(a, u, grp, s_ref, mx_ref):
        j = KV_PER_ITER * grp + u
        s = scores(pl.multiple_of(j * MOBA_BLOCK, MOBA_BLOCK), a, qas[a])
        s_ref[a, u] = s
        r = a * KV_PER_ITER + u
        mx_ref[r:r + 1, :] = jnp.max(s, axis=0, keepdims=True) + bias_ref[a, pl.ds(j, 1), :]

    def score_stage(a, grp, s_ref, mx_ref):
        for u in range(KV_PER_ITER):
            score_tile(a, u, grp, s_ref, mx_ref)

    def half_step(grp, cur, nxt, carry):
        s_ref, mx_ref = cur
        nxt_grp = jnp.minimum(grp + 1, n_groups - 1)
        out = []
        for a in range(HEADS_PER_STEP):
            m = carry[a]
            mx = jnp.max(mx_ref[a * KV_PER_ITER:(a + 1) * KV_PER_ITER, :], axis=0, keepdims=True)
            m_new = jnp.maximum(m, mx)
            acc = jnp.exp2(m - m_new) * acc_ref[a]
            for u in range(KV_PER_ITER):
                j = KV_PER_ITER * grp + u
                m_eff = m_new - bias_ref[a, pl.ds(j, 1), :]
                p = jnp.exp2(s_ref[a, u] - m_eff)
                acc = acc + _dot(vt_ref[j, a], p.astype(BF16))
            acc_ref[a] = acc
            if nxt is not None:
                score_stage(a, nxt_grp, *nxt)
            out.append(m_new)
        return out

    buf_a = (sa_ref, mxa_ref)
    buf_b = (sb_ref, mxb_ref)

    def body(tt, carry):
        carry = half_step(2 * tt, buf_a, buf_b, list(carry))
        carry = half_step(2 * tt + 1, buf_b, buf_a, carry)
        return tuple(carry)

    n_iter = (i + KV_PER_ITER - 1) // KV_PER_ITER
    for a in range(HEADS_PER_STEP):
        score_stage(a, 0, *buf_a)
    carry = lax.fori_loop(0, n_iter // 2, body, tuple(carry))

    @pl.when(n_iter % 2 == 1)
    def _():
        half_step(n_iter - 1, buf_a, None, list(carry))

    for a in range(HEADS_PER_STEP):
        lo, hi = ATTN_DH * a, ATTN_DH * (a + 1)
        acc = acc_ref[a]
        denom = acc[ATTN_DH:ATTN_DH + 1, :]
        o = (acc[0:ATTN_DH, :] * (1.0 / denom)).T
        o_ref[:, lo:hi] = (o * _silu(g_ref[:, lo:hi].astype(F32))).astype(o_ref.dtype)


def _moba_branch(proj, batch, seq):
    t = proj.shape[0]
    nb = seq // MOBA_BLOCK
    assert nb % (2 * KV_PER_ITER) == 0

    def qrow(b, h, i):
        return b * nb + i

    score_buf = pltpu.VMEM((HEADS_PER_STEP, KV_PER_ITER, MOBA_BLOCK, MOBA_BLOCK), F32)
    max_buf = pltpu.VMEM((HEADS_PER_STEP * KV_PER_ITER, MOBA_BLOCK), F32)
    return pl.pallas_call(
        _moba_kernel,
        grid=(batch, ATTN_HEADS // HEADS_PER_STEP, nb),
        in_specs=[
            pl.BlockSpec((MOBA_BLOCK, STEP_W), lambda b, h, i: (qrow(b, h, i), OFF_Q // STEP_W + h)),
            pl.BlockSpec((seq, STEP_W), lambda b, h, i: (b, OFF_K // STEP_W + h)),
            pl.BlockSpec((seq, STEP_W), lambda b, h, i: (b, OFF_V // STEP_W + h)),
            pl.BlockSpec((MOBA_BLOCK, STEP_W), lambda b, h, i: (qrow(b, h, i), OFF_GA // STEP_W + h)),
        ],
        out_specs=pl.BlockSpec((MOBA_BLOCK, STEP_W), lambda b, h, i: (qrow(b, h, i), h)),
        out_shape=jax.ShapeDtypeStruct((t, ATTN_HEADS * ATTN_DH), BF16),
        scratch_shapes=[
            pltpu.VMEM((nb, HEADS_PER_STEP, ACC_ROWS, MOBA_BLOCK), BF16),
            pltpu.VMEM((nb, STEP_W), F32),
            pltpu.VMEM((HEADS_PER_STEP, nb, MOBA_BLOCK), F32),
            score_buf, max_buf, score_buf, max_buf,
            pltpu.VMEM((HEADS_PER_STEP, ACC_ROWS, MOBA_BLOCK), F32),
        ],
        compiler_params=pltpu.CompilerParams(
            dimension_semantics=("arbitrary", "arbitrary", "arbitrary"), vmem_limit_bytes=VMEM_LIMIT),
        name="moba_branch",
    )(proj, proj, proj, proj)


MIX_TM = 1024
MIX_TN = 512
OUT_TM = 512


def _mix_kernel(yn_ref, og_ref, ws_ref, wa_ref, gs_ref, ga_ref, bs_ref, ba_ref, o_ref):
    y_ssm = _dot(yn_ref[...], ws_ref[...])
    y_att = _dot(og_ref[...], wa_ref[...])
    g_ssm = _sigmoid(gs_ref[...].astype(F32) + bs_ref[...])
    g_att = _sigmoid(ga_ref[...].astype(F32) + ba_ref[...])
    o_ref[...] = (g_ssm * y_ssm + g_att * y_att).astype(o_ref.dtype)


def _gated_merge(yn, og, w_ssm, w_att, proj, gate_bias):
    t = yn.shape[0]
    nj = D_MODEL // MIX_TN
    return pl.pallas_call(
        _mix_kernel,
        grid=(t // MIX_TM, nj),
        in_specs=[
            pl.BlockSpec((MIX_TM, D_INNER), lambda i, j: (i, 0)),
            pl.BlockSpec((MIX_TM, D_MODEL), lambda i, j: (i, 0)),
            pl.BlockSpec((D_INNER, MIX_TN), lambda i, j: (0, j)),
            pl.BlockSpec((D_MODEL, MIX_TN), lambda i, j: (0, j)),
            pl.BlockSpec((MIX_TM, MIX_TN), lambda i, j: (i, OFF_GM // MIX_TN + j)),
            pl.BlockSpec((MIX_TM, MIX_TN), lambda i, j: (i, OFF_GM // MIX_TN + nj + j)),
            pl.BlockSpec((1, MIX_TN), lambda i, j: (0, j)),
            pl.BlockSpec((1, MIX_TN), lambda i, j: (0, nj + j)),
        ],
        out_specs=pl.BlockSpec((MIX_TM, MIX_TN), lambda i, j: (i, j)),
        out_shape=jax.ShapeDtypeStruct((t, D_MODEL), BF16),
        compiler_params=pltpu.CompilerParams(
            dimension_semantics=("arbitrary", "arbitrary"), vmem_limit_bytes=VMEM_LIMIT),
        name="gated_merge",
    )(yn, og, w_ssm, w_att, proj, proj, gate_bias, gate_bias)


def _out_kernel(m_ref, w_ref, x_ref, fw_ref, o_ref):
    r = x_ref[...] + _dot(m_ref[...], w_ref[...])
    ms = jnp.mean(r * r, axis=-1, keepdims=True)
    o_ref[...] = r * lax.rsqrt(ms + EPS) * fw_ref[...]


def _output(mixed, w_out, x2, final_w):
    t = x2.shape[0]
    return pl.pallas_call(
        _out_kernel,
        grid=(t // OUT_TM,),
        in_specs=[
            pl.BlockSpec((OUT_TM, D_MODEL), lambda i: (i, 0)),
            pl.BlockSpec((D_MODEL, D_MODEL), lambda i: (0, 0)),
            pl.BlockSpec((OUT_TM, D_MODEL), lambda i: (i, 0)),
            pl.BlockSpec((1, D_MODEL), lambda i: (0, 0)),
        ],
        out_specs=pl.BlockSpec((OUT_TM, D_MODEL), lambda i: (i, 0)),
        out_shape=jax.ShapeDtypeStruct((t, D_MODEL), F32),
        compiler_params=pltpu.CompilerParams(
            dimension_semantics=("arbitrary",), vmem_limit_bytes=VMEM_LIMIT),
        name="output_norm",
    )(mixed, w_out, x2, final_w)


def _pad_lanes(v):
    return jnp.pad(v, (0, LANES - v.shape[0])).reshape(1, LANES)


def kernel(x, norm_w, w_in, conv_w, conv_b, dt_bias, A_log, D_skip, ssm_norm_w,
           w_ssm_proj, w_attn_proj, gate_bias, w_out, final_norm_w):
    batch, seq, _ = x.shape
    depth = norm_w.shape[0]
    assert depth == 1, "the block is built for a single layer"
    l = 0
    dt_lo = D_INNER + (D_INNER + 2 * N_GROUPS * N_STATE)
    dt_hi = dt_lo + N_HEADS

    head_of_channel = jnp.arange(D_INNER, dtype=jnp.int32) // HEAD_DIM
    expand = (jnp.arange(LANES, dtype=jnp.int32)[:, None] == head_of_channel[None, :]).astype(BF16)
    expand = expand.reshape(LANES, N_GROUPS, GROUP_W).transpose(1, 0, 2)
    rows = jnp.arange(CHUNK, dtype=jnp.int32)
    shifts = jnp.stack([(rows[:, None] - rows[None, :] == d).astype(BF16) for d in range(1, CONV_K)])

    x2 = x.reshape(batch * seq, D_MODEL)
    w = w_in[l]
    w_main = jnp.concatenate([w[:, :dt_lo], w[:, dt_hi:]], axis=1).astype(BF16)
    w_dt = jnp.pad(w[:, dt_lo:dt_hi], ((0, 0), (0, LANES - N_HEADS)))
    wdt_hi, wdt_lo = _split2(w_dt)

    proj, dt_raw = _in_projection(x2, norm_w[l].reshape(1, D_MODEL), w_main, wdt_hi, wdt_lo)

    yn = _ssd_branch(
        proj, dt_raw, conv_w[l], conv_b[l].reshape(1, -1),
        _pad_lanes(dt_bias[l]), _pad_lanes(A_log[l]),
        jnp.repeat(D_skip[l], HEAD_DIM).reshape(1, D_INNER),
        ssm_norm_w[l].reshape(1, D_INNER), expand, shifts, batch, seq)

    og = _moba_branch(proj, batch, seq)

    mixed = _gated_merge(yn, og, w_ssm_proj[l].astype(BF16), w_attn_proj[l].astype(BF16),
                         proj, gate_bias[l].reshape(1, -1))
    out = _output(mixed, w_out[l].astype(BF16), x2, final_norm_w.reshape(1, D_MODEL))
    return out.reshape(batch, seq, D_MODEL)
```

```python
import jax
import jax.numpy as jnp
from jax import lax
from jax.experimental import pallas as pl
from jax.experimental.pallas import tpu as pltpu

F32 = jnp.float32
BF16 = jnp.bfloat16

D_MODEL = 2048
D_INNER = 4096
HEAD_DIM = 64
N_HEADS = 64
N_GROUPS = 8
HEADS_PER_GROUP = N_HEADS // N_GROUPS
GROUP_W = D_INNER // N_GROUPS
N_STATE = 128
CONV_W = GROUP_W + 2 * N_STATE
CONV_K = 4
CHUNK = 256
HALF = CHUNK // 2
ATTN_HEADS = 16
ATTN_DH = 128
MOBA_BLOCK = 256
MOBA_TOPK = 3
EPS = 1e-6
NEG = -1e30
LOG2E = 1.4426950408889634

OFF_Z = 0
OFF_X = 4096
OFF_B = 8192
OFF_C = 9216
OFF_Q = 10240
OFF_K = 12288
OFF_V = 14336
OFF_GA = 16384
OFF_GM = 18432
PROJ_W = 22528

LANES = 128
SUBLANES = 8
BF16_ROWS = 16
VMEM_LIMIT = 56 * 1024 * 1024


def _dot(a, b):
    return jnp.dot(a, b, preferred_element_type=F32)


def _dot_nt(a, b):
    return lax.dot_general(a, b, (((1,), (1,)), ((), ())), preferred_element_type=F32)


def _dot_tn(a, b):
    return lax.dot_general(a, b, (((0,), (0,)), ((), ())), preferred_element_type=F32)


def _split2(a):
    hi = a.astype(BF16)
    lo = (a - hi.astype(F32)).astype(BF16)
    return hi, lo


def _split3(a):
    a1 = a.astype(BF16)
    r1 = a - a1.astype(F32)
    a2 = r1.astype(BF16)
    a3 = (r1 - a2.astype(F32)).astype(BF16)
    return a1, a2, a3


def _sigmoid(x):
    return 0.5 + 0.5 * jnp.tanh(0.5 * x)


def _silu(x):
    h = 0.5 * x
    return h + h * jnp.tanh(h)


IN_TM = 1024
IN_TN = 1024


def _inproj_kernel(x_ref, nw_ref, w_ref, wdh_ref, wdl_ref, o_ref, dt_ref, h_ref):
    j = pl.program_id(1)

    @pl.when(j == 0)
    def _():
        x = x_ref[...]
        ms = jnp.mean(x * x, axis=-1, keepdims=True)
        h = x * lax.rsqrt(ms + EPS) * nw_ref[...]
        hb, hl = _split2(h)
        h_ref[...] = hb
        wdh = wdh_ref[...]
        dt_ref[...] = _dot(hb, wdh) + _dot(hb, wdl_ref[...]) + _dot(hl, wdh)

    o_ref[...] = _dot(h_ref[...], w_ref[...]).astype(o_ref.dtype)


def _in_projection(x2, norm_w, w_main, wdt_hi, wdt_lo):
    t = x2.shape[0]
    grid = (t // IN_TM, PROJ_W // IN_TN)
    return pl.pallas_call(
        _inproj_kernel,
        grid=grid,
        in_specs=[
            pl.BlockSpec((IN_TM, D_MODEL), lambda i, j: (i, 0)),
            pl.BlockSpec((1, D_MODEL), lambda i, j: (0, 0)),
            pl.BlockSpec((D_MODEL, IN_TN), lambda i, j: (0, j)),
            pl.BlockSpec((D_MODEL, LANES), lambda i, j: (0, 0)),
            pl.BlockSpec((D_MODEL, LANES), lambda i, j: (0, 0)),
        ],
        out_specs=[
            pl.BlockSpec((IN_TM, IN_TN), lambda i, j: (i, j)),
            pl.BlockSpec((IN_TM, LANES), lambda i, j: (i, 0)),
        ],
        out_shape=[
            jax.ShapeDtypeStruct((t, PROJ_W), BF16),
            jax.ShapeDtypeStruct((t, LANES), F32),
        ],
        scratch_shapes=[pltpu.VMEM((IN_TM, D_MODEL), BF16)],
        compiler_params=pltpu.CompilerParams(
            dimension_semantics=("arbitrary", "arbitrary"), vmem_limit_bytes=VMEM_LIMIT),
        name="in_projection",
    )(x2, norm_w, w_main, wdt_hi, wdt_lo)


def _ssd_kernel(z_ref, x_ref, bc_ref, dtraw_ref, cw_ref, cbias_ref,
                dtb_ref, alog_ref, dskip_ref, nw_ref, e_ref, shift_ref,
                y_ref,
                state_ref, tail_ref, halo_ref, cumg_ref, cumt_ref, hq_ref, dlast_ref):
    c = pl.program_id(1)

    row = lax.broadcasted_iota(jnp.int32, (CHUNK, CHUNK), 0)
    col = lax.broadcasted_iota(jnp.int32, (CHUNK, CHUNK), 1)
    causal = row >= col

    def chunk_quantities():
        dtr = dtraw_ref[...] + dtb_ref[...]
        dt = jnp.maximum(dtr, 0.0) + jnp.log1p(jnp.exp(-jnp.abs(dtr)))
        a = dt * (-jnp.exp(alog_ref[...]))
        a1, a2, a3 = _split3(a)
        tril = causal.astype(BF16)
        cum = _dot(tril, a1) + _dot(tril, a2) + _dot(tril, a3)
        cum2 = cum * LOG2E
        cumt_ref[...] = cum2.T
        for gg in range(N_GROUPS):
            sh = (LANES - HEADS_PER_GROUP * gg) % LANES
            cumg_ref[gg] = cum2 if sh == 0 else pltpu.roll(cum2, sh, axis=1)
        last = cum[CHUNK - 1:CHUNK, :]
        hq_ref[0] = dt.astype(BF16)
        hq_ref[1] = jnp.exp(cum).astype(BF16)
        hq_ref[2] = jnp.exp(last - cum).astype(BF16)
        dlast_ref[...] = jnp.broadcast_to(jnp.exp(last), (SUBLANES, LANES))

    chunk_quantities()

    @pl.when(c == 0)
    def _():
        tail_ref[...] = jnp.zeros(tail_ref.shape, F32)
        state_ref[...] = jnp.zeros(state_ref.shape, F32)

    causal_top = causal[0:HALF, 0:HALF]
    lane = lax.broadcasted_iota(jnp.int32, (CHUNK, LANES), 1)

    def group(g, carry):
        gx = pl.multiple_of(g * GROUP_W, GROUP_W)
        gb = pl.multiple_of(g * N_STATE, N_STATE)
        gc = pl.multiple_of(N_GROUPS * N_STATE + g * N_STATE, N_STATE)

        def conv_channels(ref):
            return jnp.concatenate([ref[:, pl.ds(gx, GROUP_W)],
                                    ref[:, pl.ds(pl.multiple_of(D_INNER + gb, N_STATE), N_STATE)],
                                    ref[:, pl.ds(pl.multiple_of(D_INNER + gc, N_STATE), N_STATE)]], axis=1)

        xin = jnp.concatenate([x_ref[:, pl.ds(gx, GROUP_W)], bc_ref[:, pl.ds(gb, N_STATE)],
                               bc_ref[:, pl.ds(gc, N_STATE)]], axis=1)
        xin_f = xin.astype(F32)
        cw = conv_channels(cw_ref)
        cbias = conv_channels(cbias_ref)
        halo_ref[0:SUBLANES, :] = tail_ref[g]
        halo_ref[SUBLANES:2 * SUBLANES, :] = jnp.zeros((SUBLANES, CONV_W), F32)
        tail_ref[g] = xin_f[CHUNK - SUBLANES:CHUNK, :]
        acc = cbias + cw[CONV_K - 1:CONV_K, :] * xin_f
        cor = jnp.zeros((SUBLANES, CONV_W), F32)
        for d in range(1, CONV_K):
            wk = cw[CONV_K - 1 - d:CONV_K - d, :]
            acc = acc + wk * _dot(shift_ref[d - 1], xin)
            cor = cor + wk * halo_ref[SUBLANES - d:2 * SUBLANES - d, :]
        acc = jnp.concatenate([acc[0:SUBLANES] + cor, acc[SUBLANES:]], axis=0)
        act = _silu(acc)
        xc = act[:, 0:GROUP_W]
        bmat = act[:, GROUP_W:GROUP_W + N_STATE].astype(BF16)
        cmat = act[:, GROUP_W + N_STATE:CONV_W].astype(BF16)

        eg = e_ref[g]
        e_dt = _dot(hq_ref[0], eg)
        e_cum = _dot(hq_ref[1], eg)
        e_end = _dot(hq_ref[2], eg)
        dh, dlo = _split2(dlast_ref[...])
        dl = _dot(dh, eg) + _dot(dlo, eg)

        cb = _dot_nt(cmat, bmat)
        cb_top = cb[0:HALF, 0:HALF]
        cb_bot = cb[HALF:CHUNK, :]
        xdt = xc * e_dt
        xdt_b = xdt.astype(BF16)
        xde_b = (xdt * e_end).astype(BF16)

        cg = cumg_ref[g]
        ys = []
        for pair in range(HEADS_PER_GROUP // 2):
            xp = xdt_b[:, LANES * pair:LANES * (pair + 1)]
            top = None
            bot = None
            for sub in range(2):
                hh = 2 * pair + sub
                colv = cg[:, hh:hh + 1]
                rowv = cumt_ref[pl.ds(g * HEADS_PER_GROUP + hh, 1), :]
                seg_t = jnp.where(causal_top, colv[0:HALF] - rowv[:, 0:HALF], NEG)
                seg_l = colv[HALF:CHUNK] - rowv[:, 0:HALF]
                seg_r = jnp.where(causal_top, colv[HALF:CHUNK] - rowv[:, HALF:CHUNK], NEG)
                w_t = (cb_top * jnp.exp2(seg_t)).astype(BF16)
                w_b = (cb_bot * jnp.exp2(jnp.concatenate([seg_l, seg_r], axis=1))).astype(BF16)
                in_head = (lane >= HEAD_DIM * sub) & (lane < HEAD_DIM * (sub + 1))
                rhs = jnp.where(in_head, xp, jnp.zeros_like(xp))
                t_t = _dot(w_t, rhs[0:HALF])
                t_b = _dot(w_b, rhs)
                top = t_t if top is None else top + t_t
                bot = t_b if bot is None else bot + t_b
            ys.append(jnp.concatenate([top, bot], axis=0))
        y = jnp.concatenate(ys, axis=1)

        st = state_ref[g]
        y = y + _dot(cmat, st.astype(BF16)) * e_cum
        y = y + dskip_ref[:, pl.ds(gx, GROUP_W)] * xc
        y = y * _silu(z_ref[:, pl.ds(gx, GROUP_W)].astype(F32))
        ms = jnp.mean(y * y, axis=-1, keepdims=True)
        y = y * lax.rsqrt(ms + EPS) * nw_ref[:, pl.ds(gx, GROUP_W)]
        y_ref[:, pl.ds(gx, GROUP_W)] = y.astype(y_ref.dtype)

        state_ref[g] = st * dl[0:1, :] + _dot_tn(bmat, xde_b)
        return carry

    def group_pair(t, carry):
        group(2 * t, carry)
        return group(2 * t + 1, carry)

    lax.fori_loop(0, N_GROUPS // 2, group_pair, 0)


def _ssd_branch(proj, dt_raw, conv_w, conv_b, dtb_pad, alog_pad, dskip_exp, ssm_norm_w, expand, shifts,
                batch, seq):
    t = proj.shape[0]
    nc = seq // CHUNK

    bc_w = 2 * N_GROUPS * N_STATE
    conv_ch = D_INNER + bc_w

    def rows(b, c):
        return b * nc + c

    in_specs = [
        pl.BlockSpec((CHUNK, D_INNER), lambda b, c: (rows(b, c), OFF_Z // D_INNER)),
        pl.BlockSpec((CHUNK, D_INNER), lambda b, c: (rows(b, c), OFF_X // D_INNER)),
        pl.BlockSpec((CHUNK, bc_w), lambda b, c: (rows(b, c), OFF_B // bc_w)),
        pl.BlockSpec((CHUNK, LANES), lambda b, c: (rows(b, c), 0)),
        pl.BlockSpec((CONV_K, conv_ch), lambda b, c: (0, 0)),
        pl.BlockSpec((1, conv_ch), lambda b, c: (0, 0)),
        pl.BlockSpec((1, LANES), lambda b, c: (0, 0)),
        pl.BlockSpec((1, LANES), lambda b, c: (0, 0)),
        pl.BlockSpec((1, D_INNER), lambda b, c: (0, 0)),
        pl.BlockSpec((1, D_INNER), lambda b, c: (0, 0)),
        pl.BlockSpec((N_GROUPS, LANES, GROUP_W), lambda b, c: (0, 0, 0)),
        pl.BlockSpec((CONV_K - 1, CHUNK, CHUNK), lambda b, c: (0, 0, 0)),
    ]
    return pl.pallas_call(
        _ssd_kernel,
        grid=(batch, nc),
        in_specs=in_specs,
        out_specs=pl.BlockSpec((CHUNK, D_INNER), lambda b, c: (rows(b, c), 0)),
        out_shape=jax.ShapeDtypeStruct((t, D_INNER), BF16),
        scratch_shapes=[
            pltpu.VMEM((N_GROUPS, N_STATE, GROUP_W), F32),
            pltpu.VMEM((N_GROUPS, SUBLANES, CONV_W), F32),
            pltpu.VMEM((2 * SUBLANES, CONV_W), F32),
            pltpu.VMEM((N_GROUPS, CHUNK, LANES), F32),
            pltpu.VMEM((LANES, CHUNK), F32),
            pltpu.VMEM((3, CHUNK, LANES), BF16),
            pltpu.VMEM((SUBLANES, LANES), F32),
        ],
        compiler_params=pltpu.CompilerParams(
            dimension_semantics=("arbitrary", "arbitrary"), vmem_limit_bytes=VMEM_LIMIT),
        name="ssd_branch",
    )(proj, proj, proj, dt_raw, conv_w, conv_b, dtb_pad, alog_pad, dskip_exp, ssm_norm_w, expand, shifts)


HEADS_PER_STEP = 4
STEP_W = HEADS_PER_STEP * ATTN_DH
KV_PER_ITER = 2
ACC_ROWS = ATTN_DH + BF16_ROWS
EXP2_SCALE = (ATTN_DH ** -0.5) * LOG2E


def _moba_kernel(q_ref, k_ref, v_ref, g_ref, o_ref,
                 vt_ref, kmean_ref, bias_ref, sa_ref, mxa_ref, sb_ref, mxb_ref, acc_ref, m_ref):
    i = pl.program_id(2)
    seq = k_ref.shape[0]
    nb = seq // MOBA_BLOCK
    n_groups = nb // KV_PER_ITER

    @pl.when(i == 0)
    def _():
        def prep(jb, carry):
            off = pl.multiple_of(jb * MOBA_BLOCK, MOBA_BLOCK)
            vt = v_ref[pl.ds(off, MOBA_BLOCK), :].astype(F32).T.astype(BF16)
            for a in range(HEADS_PER_STEP):
                vt_ref[jb, a, 0:ATTN_DH, :] = vt[ATTN_DH * a:ATTN_DH * (a + 1), :]
                vt_ref[jb, a, ATTN_DH:ACC_ROWS, :] = jnp.ones((BF16_ROWS, MOBA_BLOCK), BF16)
            kb = k_ref[pl.ds(off, MOBA_BLOCK), :].astype(F32)
            kmean_ref[pl.ds(jb, 1), :] = jnp.mean(kb, axis=0, keepdims=True)
            return carry
        lax.fori_loop(0, nb, prep, 0)

    km_hi, km_lo = _split2(kmean_ref[...])
    blk = lax.broadcasted_iota(jnp.int32, (nb, MOBA_BLOCK), 0)
    past = blk < i
    key = lax.broadcasted_iota(jnp.int32, (MOBA_BLOCK, MOBA_BLOCK), 0)
    qry = lax.broadcasted_iota(jnp.int32, (MOBA_BLOCK, MOBA_BLOCK), 1)
    own = pl.multiple_of(i * MOBA_BLOCK, MOBA_BLOCK)

    def scores(off, a, qa):
        kk = k_ref[pl.ds(off, MOBA_BLOCK), ATTN_DH * a:ATTN_DH * (a + 1)]
        return _dot_nt(jnp.concatenate([kk, kk], axis=1), qa)

    heads = range(HEADS_PER_STEP)
    qs = [q_ref[:, ATTN_DH * a:ATTN_DH * (a + 1)] for a in heads]
    qas = []
    for a in heads:
        q_hi, q_lo = _split2(qs[a].astype(F32) * EXP2_SCALE)
        qas.append(jnp.concatenate([q_hi, q_lo], axis=1))

    gates = []
    for a in heads:
        lo, hi = ATTN_DH * a, ATTN_DH * (a + 1)
        gate = _dot_nt(km_hi[:, lo:hi], qs[a]) + _dot_nt(km_lo[:, lo:hi], qs[a])
        gates.append(jnp.where(past, gate, NEG))
    sels = [jnp.zeros((nb, MOBA_BLOCK), jnp.bool_) for a in heads]
    for _ in range(MOBA_TOPK):
        for a in heads:
            mx = jnp.max(gates[a], axis=0, keepdims=True)
            idx = jnp.min(jnp.where(gates[a] == mx, blk, 4 * nb), axis=0, keepdims=True)
            hit = blk == idx
            sels[a] = sels[a] | (hit & past)
            gates[a] = jnp.where(hit, -jnp.inf, gates[a])
    for a in heads:
        bias_ref[a] = jnp.where(sels[a], 0.0, NEG)

    for a in heads:
        s = jnp.where(key <= qry, scores(own, a, qas[a]), NEG)
        m = jnp.max(s, axis=0, keepdims=True)
        p = jnp.exp2(s - m)
        acc_ref[a] = _dot(vt_ref[i, a], p.astype(BF16))
        m_ref[a:a + 1, :] = m

    def score_tile(a, u, grp, s_ref, mx_ref):
        j = KV_PER_ITER * grp + u
        s = scores(pl.multiple_of(j * MOBA_BLOCK, MOBA_BLOCK), a, qas[a])
        s_ref[a, u] = s
        r = a * KV_PER_ITER + u
        mx_ref[r:r + 1, :] = jnp.max(s, axis=0, keepdims=True) + bias_ref[a, pl.ds(j, 1), :]

    def score_stage(a, grp, s_ref, mx_ref):
        for u in range(KV_PER_ITER):
            score_tile(a, u, grp, s_ref, mx_ref)

    def half_step(grp, cur, nxt):
        s_ref, mx_ref = cur
        nxt_grp = jnp.minimum(grp + 1, n_groups - 1)
        for a in range(HEADS_PER_STEP):
            m = m_ref[a:a + 1, :]
            mx = jnp.max(mx_ref[a * KV_PER_ITER:(a + 1) * KV_PER_ITER, :], axis=0, keepdims=True)
            m_new = jnp.maximum(m, mx)
            acc = jnp.exp2(m - m_new) * acc_ref[a]
            for u in range(KV_PER_ITER):
                j = KV_PER_ITER * grp + u
                m_eff = m_new - bias_ref[a, pl.ds(j, 1), :]
                p = jnp.exp2(s_ref[a, u] - m_eff)
                acc = acc + _dot(vt_ref[j, a], p.astype(BF16))
            acc_ref[a] = acc
            m_ref[a:a + 1, :] = m_new
            if nxt is not None:
                score_stage(a, nxt_grp, *nxt)

    buf_a = (sa_ref, mxa_ref)
    buf_b = (sb_ref, mxb_ref)

    def pair(grp):
        half_step(grp, buf_a, buf_b)
        half_step(grp + 1, buf_b, buf_a)

    def quad(tt, carry):
        pair(4 * tt)
        pair(4 * tt + 2)
        return carry

    n_iter = (i + KV_PER_ITER - 1) // KV_PER_ITER
    for a in range(HEADS_PER_STEP):
        score_stage(a, 0, *buf_a)
    lax.fori_loop(0, n_iter // 4, quad, 0)
    rest = n_iter % 4

    @pl.when(rest >= 2)
    def _():
        pair(n_iter - rest)

    @pl.when(rest % 2 == 1)
    def _():
        half_step(n_iter - 1, buf_a, None)

    for a in range(HEADS_PER_STEP):
        lo, hi = ATTN_DH * a, ATTN_DH * (a + 1)
        acc = acc_ref[a]
        denom = acc[ATTN_DH:ATTN_DH + 1, :]
        o = (acc[0:ATTN_DH, :] * (1.0 / denom)).T
        o_ref[:, lo:hi] = (o * _silu(g_ref[:, lo:hi].astype(F32))).astype(o_ref.dtype)


def _moba_branch(proj, batch, seq):
    t = proj.shape[0]
    nb = seq // MOBA_BLOCK
    assert nb % (2 * KV_PER_ITER) == 0

    def qrow(b, h, i):
        return b * nb + i

    score_buf = pltpu.VMEM((HEADS_PER_STEP, KV_PER_ITER, MOBA_BLOCK, MOBA_BLOCK), F32)
    max_buf = pltpu.VMEM((HEADS_PER_STEP * KV_PER_ITER, MOBA_BLOCK), F32)
    return pl.pallas_call(
        _moba_kernel,
        grid=(batch, ATTN_HEADS // HEADS_PER_STEP, nb),
        in_specs=[
            pl.BlockSpec((MOBA_BLOCK, STEP_W), lambda b, h, i: (qrow(b, h, i), OFF_Q // STEP_W + h)),
            pl.BlockSpec((seq, STEP_W), lambda b, h, i: (b, OFF_K // STEP_W + h)),
            pl.BlockSpec((seq, STEP_W), lambda b, h, i: (b, OFF_V // STEP_W + h)),
            pl.BlockSpec((MOBA_BLOCK, STEP_W), lambda b, h, i: (qrow(b, h, i), OFF_GA // STEP_W + h)),
        ],
        out_specs=pl.BlockSpec((MOBA_BLOCK, STEP_W), lambda b, h, i: (qrow(b, h, i), h)),
        out_shape=jax.ShapeDtypeStruct((t, ATTN_HEADS * ATTN_DH), BF16),
        scratch_shapes=[
            pltpu.VMEM((nb, HEADS_PER_STEP, ACC_ROWS, MOBA_BLOCK), BF16),
            pltpu.VMEM((nb, STEP_W), F32),
            pltpu.VMEM((HEADS_PER_STEP, nb, MOBA_BLOCK), F32),
            score_buf, max_buf, score_buf, max_buf,
            pltpu.VMEM((HEADS_PER_STEP, ACC_ROWS, MOBA_BLOCK), F32),
            pltpu.VMEM((SUBLANES, MOBA_BLOCK), F32),
        ],
        compiler_params=pltpu.CompilerParams(
            dimension_semantics=("arbitrary", "arbitrary", "arbitrary"), vmem_limit_bytes=VMEM_LIMIT),
        name="moba_branch",
    )(proj, proj, proj, proj)


MIX_TM = 1024
MIX_TN = 512
OUT_TM = 512


def _mix_kernel(yn_ref, og_ref, ws_ref, wa_ref, gs_ref, ga_ref, bs_ref, ba_ref, o_ref):
    y_ssm = _dot(yn_ref[...], ws_ref[...])
    y_att = _dot(og_ref[...], wa_ref[...])
    g_ssm = _sigmoid(gs_ref[...].astype(F32) + bs_ref[...])
    g_att = _sigmoid(ga_ref[...].astype(F32) + ba_ref[...])
    o_ref[...] = (g_ssm * y_ssm + g_att * y_att).astype(o_ref.dtype)


def _gated_merge(yn, og, w_ssm, w_att, proj, gate_bias):
    t = yn.shape[0]
    nj = D_MODEL // MIX_TN
    return pl.pallas_call(
        _mix_kernel,
        grid=(t // MIX_TM, nj),
        in_specs=[
            pl.BlockSpec((MIX_TM, D_INNER), lambda i, j: (i, 0)),
            pl.BlockSpec((MIX_TM, D_MODEL), lambda i, j: (i, 0)),
            pl.BlockSpec((D_INNER, MIX_TN), lambda i, j: (0, j)),
            pl.BlockSpec((D_MODEL, MIX_TN), lambda i, j: (0, j)),
            pl.BlockSpec((MIX_TM, MIX_TN), lambda i, j: (i, OFF_GM // MIX_TN + j)),
            pl.BlockSpec((MIX_TM, MIX_TN), lambda i, j: (i, OFF_GM // MIX_TN + nj + j)),
            pl.BlockSpec((1, MIX_TN), lambda i, j: (0, j)),
            pl.BlockSpec((1, MIX_TN), lambda i, j: (0, nj + j)),
        ],
        out_specs=pl.BlockSpec((MIX_TM, MIX_TN), lambda i, j: (i, j)),
        out_shape=jax.ShapeDtypeStruct((t, D_MODEL), BF16),
        compiler_params=pltpu.CompilerParams(
            dimension_semantics=("arbitrary", "arbitrary"), vmem_limit_bytes=VMEM_LIMIT),
        name="gated_merge",
    )(yn, og, w_ssm, w_att, proj, proj, gate_bias, gate_bias)


def _out_kernel(m_ref, w_ref, x_ref, fw_ref, o_ref):
    r = x_ref[...] + _dot(m_ref[...], w_ref[...])
    ms = jnp.mean(r * r, axis=-1, keepdims=True)
    o_ref[...] = r * lax.rsqrt(ms + EPS) * fw_ref[...]


def _output(mixed, w_out, x2, final_w):
    t = x2.shape[0]
    return pl.pallas_call(
        _out_kernel,
        grid=(t // OUT_TM,),
        in_specs=[
            pl.BlockSpec((OUT_TM, D_MODEL), lambda i: (i, 0)),
            pl.BlockSpec((D_MODEL, D_MODEL), lambda i: (0, 0)),
            pl.BlockSpec((OUT_TM, D_MODEL), lambda i: (i, 0)),
            pl.BlockSpec((1, D_MODEL), lambda i: (0, 0)),
        ],
        out_specs=pl.BlockSpec((OUT_TM, D_MODEL), lambda i: (i, 0)),
        out_shape=jax.ShapeDtypeStruct((t, D_MODEL), F32),
        compiler_params=pltpu.CompilerParams(
            dimension_semantics=("arbitrary",), vmem_limit_bytes=VMEM_LIMIT),
        name="output_norm",
    )(mixed, w_out, x2, final_w)


def _pad_lanes(v):
    return jnp.pad(v, (0, LANES - v.shape[0])).reshape(1, LANES)


def kernel(x, norm_w, w_in, conv_w, conv_b, dt_bias, A_log, D_skip, ssm_norm_w,
           w_ssm_proj, w_attn_proj, gate_bias, w_out, final_norm_w):
    batch, seq, _ = x.shape
    depth = norm_w.shape[0]
    assert depth == 1, "the block is built for a single layer"
    l = 0
    dt_lo = D_INNER + (D_INNER + 2 * N_GROUPS * N_STATE)
    dt_hi = dt_lo + N_HEADS

    head_of_channel = jnp.arange(D_INNER, dtype=jnp.int32) // HEAD_DIM
    expand = (jnp.arange(LANES, dtype=jnp.int32)[:, None] == head_of_channel[None, :]).astype(BF16)
    expand = expand.reshape(LANES, N_GROUPS, GROUP_W).transpose(1, 0, 2)
    rows = jnp.arange(CHUNK, dtype=jnp.int32)
    shifts = jnp.stack([(rows[:, None] - rows[None, :] == d).astype(BF16) for d in range(1, CONV_K)])

    x2 = x.reshape(batch * seq, D_MODEL)
    w = w_in[l]
    w_main = jnp.concatenate([w[:, :dt_lo], w[:, dt_hi:]], axis=1).astype(BF16)
    w_dt = jnp.pad(w[:, dt_lo:dt_hi], ((0, 0), (0, LANES - N_HEADS)))
    wdt_hi, wdt_lo = _split2(w_dt)

    proj, dt_raw = _in_projection(x2, norm_w[l].reshape(1, D_MODEL), w_main, wdt_hi, wdt_lo)

    yn = _ssd_branch(
        proj, dt_raw, conv_w[l], conv_b[l].reshape(1, -1),
        _pad_lanes(dt_bias[l]), _pad_lanes(A_log[l]),
        jnp.repeat(D_skip[l], HEAD_DIM).reshape(1, D_INNER),
        ssm_norm_w[l].reshape(1, D_INNER), expand, shifts, batch, seq)

    og = _moba_branch(proj, batch, seq)

    mixed = _gated_merge(yn, og, w_ssm_proj[l].astype(BF16), w_attn_proj[l].astype(BF16),
                         proj, gate_bias[l].reshape(1, -1))
    out = _output(mixed, w_out[l].astype(BF16), x2, final_norm_w.reshape(1, D_MODEL))
    return out.reshape(batch, seq, D_MODEL)
```

```python
import jax
import jax.numpy as jnp
from jax import lax
from jax.experimental import pallas as pl
from jax.experimental.pallas import tpu as pltpu

F32 = jnp.float32
BF16 = jnp.bfloat16

D_MODEL = 2048
D_INNER = 4096
HEAD_DIM = 64
N_HEADS = 64
N_GROUPS = 8
HEADS_PER_GROUP = N_HEADS // N_GROUPS
GROUP_W = D_INNER // N_GROUPS
N_STATE = 128
CONV_W = GROUP_W + 2 * N_STATE
CONV_K = 4
CHUNK = 256
HALF = CHUNK // 2
ATTN_HEADS = 16
ATTN_DH = 128
MOBA_BLOCK = 256
MOBA_TOPK = 3
EPS = 1e-6
NEG = -1e30
LOG2E = 1.4426950408889634

OFF_Z = 0
OFF_X = 4096
OFF_B = 8192
OFF_C = 9216
OFF_Q = 10240
OFF_K = 12288
OFF_V = 14336
OFF_GA = 16384
OFF_GM = 18432
PROJ_W = 22528

LANES = 128
SUBLANES = 8
BF16_ROWS = 16
VMEM_LIMIT = 56 * 1024 * 1024


def _dot(a, b):
    return jnp.dot(a, b, preferred_element_type=F32)


def _dot_nt(a, b):
    return lax.dot_general(a, b, (((1,), (1,)), ((), ())), preferred_element_type=F32)


def _dot_tn(a, b):
    return lax.dot_general(a, b, (((0,), (0,)), ((), ())), preferred_element_type=F32)


def _split2(a):
    hi = a.astype(BF16)
    lo = (a - hi.astype(F32)).astype(BF16)
    return hi, lo


def _split3(a):
    a1 = a.astype(BF16)
    r1 = a - a1.astype(F32)
    a2 = r1.astype(BF16)
    a3 = (r1 - a2.astype(F32)).astype(BF16)
    return a1, a2, a3


def _sigmoid(x):
    return 0.5 + 0.5 * jnp.tanh(0.5 * x)


def _silu(x):
    h = 0.5 * x
    return h + h * jnp.tanh(h)


IN_TM = 1024
IN_TN = 1024


def _inproj_kernel(x_ref, nw_ref, w_ref, wdh_ref, wdl_ref, o_ref, dt_ref, h_ref):
    j = pl.program_id(1)

    @pl.when(j == 0)
    def _():
        x = x_ref[...]
        ms = jnp.mean(x * x, axis=-1, keepdims=True)
        h = x * lax.rsqrt(ms + EPS) * nw_ref[...]
        hb, hl = _split2(h)
        h_ref[...] = hb
        wdh = wdh_ref[...]
        dt_ref[...] = _dot(hb, wdh) + _dot(hb, wdl_ref[...]) + _dot(hl, wdh)

    o_ref[...] = _dot(h_ref[...], w_ref[...]).astype(o_ref.dtype)


def _in_projection(x2, norm_w, w_main, wdt_hi, wdt_lo):
    t = x2.shape[0]
    grid = (t // IN_TM, PROJ_W // IN_TN)
    return pl.pallas_call(
        _inproj_kernel,
        grid=grid,
        in_specs=[
            pl.BlockSpec((IN_TM, D_MODEL), lambda i, j: (i, 0)),
            pl.BlockSpec((1, D_MODEL), lambda i, j: (0, 0)),
            pl.BlockSpec((D_MODEL, IN_TN), lambda i, j: (0, j)),
            pl.BlockSpec((D_MODEL, LANES), lambda i, j: (0, 0)),
            pl.BlockSpec((D_MODEL, LANES), lambda i, j: (0, 0)),
        ],
        out_specs=[
            pl.BlockSpec((IN_TM, IN_TN), lambda i, j: (i, j)),
            pl.BlockSpec((IN_TM, LANES), lambda i, j: (i, 0)),
        ],
        out_shape=[
            jax.ShapeDtypeStruct((t, PROJ_W), BF16),
            jax.ShapeDtypeStruct((t, LANES), F32),
        ],
        scratch_shapes=[pltpu.VMEM((IN_TM, D_MODEL), BF16)],
        compiler_params=pltpu.CompilerParams(
            dimension_semantics=("arbitrary", "arbitrary"), vmem_limit_bytes=VMEM_LIMIT),
        name="in_projection",
    )(x2, norm_w, w_main, wdt_hi, wdt_lo)


SSD_GROUPS_PER_TRIP = 2


def _ssd_kernel(z_ref, x_ref, bc_ref, dtraw_ref, cw_ref, cbias_ref,
                dtb_ref, alog_ref, dskip_ref, nw_ref, e_ref, shift_ref,
                y_ref,
                state_ref, tail_ref, halo_ref, cumg_ref, cumt_ref, hq_ref, dlast_ref):
    c = pl.program_id(1)

    row = lax.broadcasted_iota(jnp.int32, (CHUNK, CHUNK), 0)
    col = lax.broadcasted_iota(jnp.int32, (CHUNK, CHUNK), 1)
    causal = row >= col

    def chunk_quantities():
        dtr = dtraw_ref[...] + dtb_ref[...]
        dt = jnp.maximum(dtr, 0.0) + jnp.log1p(jnp.exp(-jnp.abs(dtr)))
        a = dt * (-jnp.exp(alog_ref[...]))
        a1, a2, a3 = _split3(a)
        tril = causal.astype(BF16)
        cum = _dot(tril, a1) + _dot(tril, a2) + _dot(tril, a3)
        cum2 = cum * LOG2E
        cumt_ref[...] = cum2.T
        for gg in range(N_GROUPS):
            sh = (LANES - HEADS_PER_GROUP * gg) % LANES
            cumg_ref[gg] = cum2 if sh == 0 else pltpu.roll(cum2, sh, axis=1)
        last = cum[CHUNK - 1:CHUNK, :]
        hq_ref[0] = dt.astype(BF16)
        hq_ref[1] = jnp.exp(cum).astype(BF16)
        hq_ref[2] = jnp.exp(last - cum).astype(BF16)
        dlast_ref[...] = jnp.broadcast_to(jnp.exp(last), (SUBLANES, LANES))

    chunk_quantities()

    @pl.when(c == 0)
    def _():
        tail_ref[...] = jnp.zeros(tail_ref.shape, F32)
        state_ref[...] = jnp.zeros(state_ref.shape, F32)

    causal_top = causal[0:HALF, 0:HALF]
    lane = lax.broadcasted_iota(jnp.int32, (CHUNK, LANES), 1)

    def group(g, carry):
        gx = pl.multiple_of(g * GROUP_W, GROUP_W)
        gb = pl.multiple_of(g * N_STATE, N_STATE)
        gc = pl.multiple_of(N_GROUPS * N_STATE + g * N_STATE, N_STATE)

        def conv_channels(ref):
            return jnp.concatenate([ref[:, pl.ds(gx, GROUP_W)],
                                    ref[:, pl.ds(pl.multiple_of(D_INNER + gb, N_STATE), N_STATE)],
                                    ref[:, pl.ds(pl.multiple_of(D_INNER + gc, N_STATE), N_STATE)]], axis=1)

        xin = jnp.concatenate([x_ref[:, pl.ds(gx, GROUP_W)], bc_ref[:, pl.ds(gb, N_STATE)],
                               bc_ref[:, pl.ds(gc, N_STATE)]], axis=1)
        xin_f = xin.astype(F32)
        cw = conv_channels(cw_ref)
        cbias = conv_channels(cbias_ref)
        halo_ref[0:SUBLANES, :] = tail_ref[g]
        halo_ref[SUBLANES:2 * SUBLANES, :] = jnp.zeros((SUBLANES, CONV_W), F32)
        tail_ref[g] = xin_f[CHUNK - SUBLANES:CHUNK, :]
        acc = cbias + cw[CONV_K - 1:CONV_K, :] * xin_f
        cor = jnp.zeros((SUBLANES, CONV_W), F32)
        for d in range(1, CONV_K):
            wk = cw[CONV_K - 1 - d:CONV_K - d, :]
            acc = acc + wk * _dot(shift_ref[d - 1], xin)
            cor = cor + wk * halo_ref[SUBLANES - d:2 * SUBLANES - d, :]
        acc = jnp.concatenate([acc[0:SUBLANES] + cor, acc[SUBLANES:]], axis=0)
        act = _silu(acc)
        xc = act[:, 0:GROUP_W]
        bmat = act[:, GROUP_W:GROUP_W + N_STATE].astype(BF16)
        cmat = act[:, GROUP_W + N_STATE:CONV_W].astype(BF16)

        eg = e_ref[g]
        e_dt = _dot(hq_ref[0], eg)
        e_cum = _dot(hq_ref[1], eg)
        e_end = _dot(hq_ref[2], eg)
        dh, dlo = _split2(dlast_ref[...])
        dl = _dot(dh, eg) + _dot(dlo, eg)

        cb = _dot_nt(cmat, bmat)
        cb_top = cb[0:HALF, 0:HALF]
        cb_bot = cb[HALF:CHUNK, :]
        xdt = xc * e_dt
        xdt_b = xdt.astype(BF16)
        xde_b = (xdt * e_end).astype(BF16)

        cg = cumg_ref[g]
        ys = []
        for pair in range(HEADS_PER_GROUP // 2):
            xp = xdt_b[:, LANES * pair:LANES * (pair + 1)]
            top = None
            bot = None
            for sub in range(2):
                hh = 2 * pair + sub
                colv = cg[:, hh:hh + 1]
                rowv = cumt_ref[pl.ds(g * HEADS_PER_GROUP + hh, 1), :]
                seg_t = jnp.where(causal_top, colv[0:HALF] - rowv[:, 0:HALF], NEG)
                seg_l = colv[HALF:CHUNK] - rowv[:, 0:HALF]
                seg_r = jnp.where(causal_top, colv[HALF:CHUNK] - rowv[:, HALF:CHUNK], NEG)
                w_t = (cb_top * jnp.exp2(seg_t)).astype(BF16)
                w_b = (cb_bot * jnp.exp2(jnp.concatenate([seg_l, seg_r], axis=1))).astype(BF16)
                in_head = (lane >= HEAD_DIM * sub) & (lane < HEAD_DIM * (sub + 1))
                rhs = jnp.where(in_head, xp, jnp.zeros_like(xp))
                t_t = _dot(w_t, rhs[0:HALF])
                t_b = _dot(w_b, rhs)
                top = t_t if top is None else top + t_t
                bot = t_b if bot is None else bot + t_b
            ys.append(jnp.concatenate([top, bot], axis=0))
        y = jnp.concatenate(ys, axis=1)

        st = state_ref[g]
        y = y + _dot(cmat, st.astype(BF16)) * e_cum
        y = y + dskip_ref[:, pl.ds(gx, GROUP_W)] * xc
        y = y * _silu(z_ref[:, pl.ds(gx, GROUP_W)].astype(F32))
        ms = jnp.mean(y * y, axis=-1, keepdims=True)
        y = y * lax.rsqrt(ms + EPS) * nw_ref[:, pl.ds(gx, GROUP_W)]
        y_ref[:, pl.ds(gx, GROUP_W)] = y.astype(y_ref.dtype)

        state_ref[g] = st * dl[0:1, :] + _dot_tn(bmat, xde_b)
        return carry

    def group_run(t, carry):
        for v in range(SSD_GROUPS_PER_TRIP):
            group(SSD_GROUPS_PER_TRIP * t + v, carry)
        return carry

    lax.fori_loop(0, N_GROUPS // SSD_GROUPS_PER_TRIP, group_run, 0)


def _ssd_branch(proj, dt_raw, conv_w, conv_b, dtb_pad, alog_pad, dskip_exp, ssm_norm_w, expand, shifts,
                batch, seq):
    t = proj.shape[0]
    nc = seq // CHUNK

    bc_w = 2 * N_GROUPS * N_STATE
    conv_ch = D_INNER + bc_w

    def rows(b, c):
        return b * nc + c

    in_specs = [
        pl.BlockSpec((CHUNK, D_INNER), lambda b, c: (rows(b, c), OFF_Z // D_INNER)),
        pl.BlockSpec((CHUNK, D_INNER), lambda b, c: (rows(b, c), OFF_X // D_INNER)),
        pl.BlockSpec((CHUNK, bc_w), lambda b, c: (rows(b, c), OFF_B // bc_w)),
        pl.BlockSpec((CHUNK, LANES), lambda b, c: (rows(b, c), 0)),
        pl.BlockSpec((CONV_K, conv_ch), lambda b, c: (0, 0)),
        pl.BlockSpec((1, conv_ch), lambda b, c: (0, 0)),
        pl.BlockSpec((1, LANES), lambda b, c: (0, 0)),
        pl.BlockSpec((1, LANES), lambda b, c: (0, 0)),
        pl.BlockSpec((1, D_INNER), lambda b, c: (0, 0)),
        pl.BlockSpec((1, D_INNER), lambda b, c: (0, 0)),
        pl.BlockSpec((N_GROUPS, LANES, GROUP_W), lambda b, c: (0, 0, 0)),
        pl.BlockSpec((CONV_K - 1, CHUNK, CHUNK), lambda b, c: (0, 0, 0)),
    ]
    return pl.pallas_call(
        _ssd_kernel,
        grid=(batch, nc),
        in_specs=in_specs,
        out_specs=pl.BlockSpec((CHUNK, D_INNER), lambda b, c: (rows(b, c), 0)),
        out_shape=jax.ShapeDtypeStruct((t, D_INNER), BF16),
        scratch_shapes=[
            pltpu.VMEM((N_GROUPS, N_STATE, GROUP_W), F32),
            pltpu.VMEM((N_GROUPS, SUBLANES, CONV_W), F32),
            pltpu.VMEM((2 * SUBLANES, CONV_W), F32),
            pltpu.VMEM((N_GROUPS, CHUNK, LANES), F32),
            pltpu.VMEM((LANES, CHUNK), F32),
            pltpu.VMEM((3, CHUNK, LANES), BF16),
            pltpu.VMEM((SUBLANES, LANES), F32),
        ],
        compiler_params=pltpu.CompilerParams(
            dimension_semantics=("arbitrary", "arbitrary"), vmem_limit_bytes=VMEM_LIMIT),
        name="ssd_branch",
    )(proj, proj, proj, dt_raw, conv_w, conv_b, dtb_pad, alog_pad, dskip_exp, ssm_norm_w, expand, shifts)


HEADS_PER_STEP = 4
STEP_W = HEADS_PER_STEP * ATTN_DH
KV_PER_ITER = 2
GROUPS_PER_TRIP = 8
ACC_ROWS = ATTN_DH + BF16_ROWS
EXP2_SCALE = (ATTN_DH ** -0.5) * LOG2E


def _moba_kernel(q_ref, k_ref, v_ref, g_ref, o_ref,
                 vt_ref, kmean_ref, bias_ref, sa_ref, mxa_ref, sb_ref, mxb_ref, acc_ref, m_ref):
    i = pl.program_id(2)
    seq = k_ref.shape[0]
    nb = seq // MOBA_BLOCK
    n_groups = nb // KV_PER_ITER

    @pl.when(i == 0)
    def _():
        def prep(jb, carry):
            off = pl.multiple_of(jb * MOBA_BLOCK, MOBA_BLOCK)
            vt = v_ref[pl.ds(off, MOBA_BLOCK), :].astype(F32).T.astype(BF16)
            for a in range(HEADS_PER_STEP):
                vt_ref[jb, a, 0:ATTN_DH, :] = vt[ATTN_DH * a:ATTN_DH * (a + 1), :]
                vt_ref[jb, a, ATTN_DH:ACC_ROWS, :] = jnp.ones((BF16_ROWS, MOBA_BLOCK), BF16)
            kb = k_ref[pl.ds(off, MOBA_BLOCK), :].astype(F32)
            kmean_ref[pl.ds(jb, 1), :] = jnp.mean(kb, axis=0, keepdims=True)
            return carry
        lax.fori_loop(0, nb, prep, 0)

    km_hi, km_lo = _split2(kmean_ref[...])
    blk = lax.broadcasted_iota(jnp.int32, (nb, MOBA_BLOCK), 0)
    past = blk < i
    key = lax.broadcasted_iota(jnp.int32, (MOBA_BLOCK, MOBA_BLOCK), 0)
    qry = lax.broadcasted_iota(jnp.int32, (MOBA_BLOCK, MOBA_BLOCK), 1)
    own = pl.multiple_of(i * MOBA_BLOCK, MOBA_BLOCK)

    def scores(off, a, qa):
        kk = k_ref[pl.ds(off, MOBA_BLOCK), ATTN_DH * a:ATTN_DH * (a + 1)]
        return _dot_nt(jnp.concatenate([kk, kk], axis=1), qa)

    heads = range(HEADS_PER_STEP)
    qs = [q_ref[:, ATTN_DH * a:ATTN_DH * (a + 1)] for a in heads]
    qas = []
    for a in heads:
        q_hi, q_lo = _split2(qs[a].astype(F32) * EXP2_SCALE)
        qas.append(jnp.concatenate([q_hi, q_lo], axis=1))

    gates = []
    for a in heads:
        lo, hi = ATTN_DH * a, ATTN_DH * (a + 1)
        gate = _dot_nt(km_hi[:, lo:hi], qs[a]) + _dot_nt(km_lo[:, lo:hi], qs[a])
        gates.append(jnp.where(past, gate, NEG))
    sels = [jnp.zeros((nb, MOBA_BLOCK), jnp.bool_) for a in heads]
    for _ in range(MOBA_TOPK):
        for a in heads:
            mx = jnp.max(gates[a], axis=0, keepdims=True)
            idx = jnp.min(jnp.where(gates[a] == mx, blk, 4 * nb), axis=0, keepdims=True)
            hit = blk == idx
            sels[a] = sels[a] | (hit & past)
            gates[a] = jnp.where(hit, -jnp.inf, gates[a])
    for a in heads:
        bias_ref[a] = jnp.where(sels[a], 0.0, NEG)

    for a in heads:
        s = jnp.where(key <= qry, scores(own, a, qas[a]), NEG)
        m = jnp.max(s, axis=0, keepdims=True)
        p = jnp.exp2(s - m)
        acc_ref[a] = _dot(vt_ref[i, a], p.astype(BF16))
        m_ref[a:a + 1, :] = m

    def score_tile(a, u, grp, s_ref, mx_ref):
        j = KV_PER_ITER * grp + u
        s = scores(pl.multiple_of(j * MOBA_BLOCK, MOBA_BLOCK), a, qas[a])
        s_ref[a, u] = s
        r = a * KV_PER_ITER + u
        mx_ref[r:r + 1, :] = jnp.max(s, axis=0, keepdims=True) + bias_ref[a, pl.ds(j, 1), :]

    def score_stage(a, grp, s_ref, mx_ref):
        for u in range(KV_PER_ITER):
            score_tile(a, u, grp, s_ref, mx_ref)

    def half_step(grp, cur, nxt):
        s_ref, mx_ref = cur
        nxt_grp = jnp.minimum(grp + 1, n_groups - 1)
        for a in range(HEADS_PER_STEP):
            m = m_ref[a:a + 1, :]
            mx = jnp.max(mx_ref[a * KV_PER_ITER:(a + 1) * KV_PER_ITER, :], axis=0, keepdims=True)
            m_new = jnp.maximum(m, mx)
            acc = jnp.exp2(m - m_new) * acc_ref[a]
            for u in range(KV_PER_ITER):
                j = KV_PER_ITER * grp + u
                m_eff = m_new - bias_ref[a, pl.ds(j, 1), :]
                p = jnp.exp2(s_ref[a, u] - m_eff)
                acc = acc + _dot(vt_ref[j, a], p.astype(BF16))
            acc_ref[a] = acc
            m_ref[a:a + 1, :] = m_new
            if nxt is not None:
                score_stage(a, nxt_grp, *nxt)

    buf_a = (sa_ref, mxa_ref)
    buf_b = (sb_ref, mxb_ref)

    def pair(grp):
        half_step(grp, buf_a, buf_b)
        half_step(grp + 1, buf_b, buf_a)

    def trip(tt, carry):
        for v in range(0, GROUPS_PER_TRIP, 2):
            pair(GROUPS_PER_TRIP * tt + v)
        return carry

    n_iter = (i + KV_PER_ITER - 1) // KV_PER_ITER
    for a in range(HEADS_PER_STEP):
        score_stage(a, 0, *buf_a)
    lax.fori_loop(0, n_iter // GROUPS_PER_TRIP, trip, 0)
    rest = n_iter % GROUPS_PER_TRIP

    w = GROUPS_PER_TRIP // 2
    while w >= 2:
        @pl.when((rest // w) % 2 == 1)
        def _(w=w):
            for v in range(0, w, 2):
                pair(n_iter - rest % (2 * w) + v)
        w //= 2

    @pl.when(rest % 2 == 1)
    def _():
        half_step(n_iter - 1, buf_a, None)

    for a in range(HEADS_PER_STEP):
        lo, hi = ATTN_DH * a, ATTN_DH * (a + 1)
        acc = acc_ref[a]
        denom = acc[ATTN_DH:ATTN_DH + 1, :]
        o = (acc[0:ATTN_DH, :] * (1.0 / denom)).T
        o_ref[:, lo:hi] = (o * _silu(g_ref[:, lo:hi].astype(F32))).astype(o_ref.dtype)


def _moba_branch(proj, batch, seq):
    t = proj.shape[0]
    nb = seq // MOBA_BLOCK
    assert nb % (2 * KV_PER_ITER) == 0

    def qrow(b, h, i):
        return b * nb + i

    score_buf = pltpu.VMEM((HEADS_PER_STEP, KV_PER_ITER, MOBA_BLOCK, MOBA_BLOCK), F32)
    max_buf = pltpu.VMEM((HEADS_PER_STEP * KV_PER_ITER, MOBA_BLOCK), F32)
    return pl.pallas_call(
        _moba_kernel,
        grid=(batch, ATTN_HEADS // HEADS_PER_STEP, nb),
        in_specs=[
            pl.BlockSpec((MOBA_BLOCK, STEP_W), lambda b, h, i: (qrow(b, h, i), OFF_Q // STEP_W + h)),
            pl.BlockSpec((seq, STEP_W), lambda b, h, i: (b, OFF_K // STEP_W + h)),
            pl.BlockSpec((seq, STEP_W), lambda b, h, i: (b, OFF_V // STEP_W + h)),
            pl.BlockSpec((MOBA_BLOCK, STEP_W), lambda b, h, i: (qrow(b, h, i), OFF_GA // STEP_W + h)),
        ],
        out_specs=pl.BlockSpec((MOBA_BLOCK, STEP_W), lambda b, h, i: (qrow(b, h, i), h)),
        out_shape=jax.ShapeDtypeStruct((t, ATTN_HEADS * ATTN_DH), BF16),
        scratch_shapes=[
            pltpu.VMEM((nb, HEADS_PER_STEP, ACC_ROWS, MOBA_BLOCK), BF16),
            pltpu.VMEM((nb, STEP_W), F32),
            pltpu.VMEM((HEADS_PER_STEP, nb, MOBA_BLOCK), F32),
            score_buf, max_buf, score_buf, max_buf,
            pltpu.VMEM((HEADS_PER_STEP, ACC_ROWS, MOBA_BLOCK), F32),
            pltpu.VMEM((SUBLANES, MOBA_BLOCK), F32),
        ],
        compiler_params=pltpu.CompilerParams(
            dimension_semantics=("arbitrary", "arbitrary", "arbitrary"), vmem_limit_bytes=VMEM_LIMIT),
        name="moba_branch",
    )(proj, proj, proj, proj)


MIX_TM = 1024
MIX_TN = 512
OUT_TM = 512


def _mix_kernel(yn_ref, og_ref, ws_ref, wa_ref, gs_ref, ga_ref, bs_ref, ba_ref, o_ref):
    y_ssm = _dot(yn_ref[...], ws_ref[...])
    y_att = _dot(og_ref[...], wa_ref[...])
    g_ssm = _sigmoid(gs_ref[...].astype(F32) + bs_ref[...])
    g_att = _sigmoid(ga_ref[...].astype(F32) + ba_ref[...])
    o_ref[...] = (g_ssm * y_ssm + g_att * y_att).astype(o_ref.dtype)


def _gated_merge(yn, og, w_ssm, w_att, proj, gate_bias):
    t = yn.shape[0]
    nj = D_MODEL // MIX_TN
    return pl.pallas_call(
        _mix_kernel,
        grid=(t // MIX_TM, nj),
        in_specs=[
            pl.BlockSpec((MIX_TM, D_INNER), lambda i, j: (i, 0)),
            pl.BlockSpec((MIX_TM, D_MODEL), lambda i, j: (i, 0)),
            pl.BlockSpec((D_INNER, MIX_TN), lambda i, j: (0, j)),
            pl.BlockSpec((D_MODEL, MIX_TN), lambda i, j: (0, j)),
            pl.BlockSpec((MIX_TM, MIX_TN), lambda i, j: (i, OFF_GM // MIX_TN + j)),
            pl.BlockSpec((MIX_TM, MIX_TN), lambda i, j: (i, OFF_GM // MIX_TN + nj + j)),
            pl.BlockSpec((1, MIX_TN), lambda i, j: (0, j)),
            pl.BlockSpec((1, MIX_TN), lambda i, j: (0, nj + j)),
        ],
        out_specs=pl.BlockSpec((MIX_TM, MIX_TN), lambda i, j: (i, j)),
        out_shape=jax.ShapeDtypeStruct((t, D_MODEL), BF16),
        compiler_params=pltpu.CompilerParams(
            dimension_semantics=("arbitrary", "arbitrary"), vmem_limit_bytes=VMEM_LIMIT),
        name="gated_merge",
    )(yn, og, w_ssm, w_att, proj, proj, gate_bias, gate_bias)


def _out_kernel(m_ref, w_ref, x_ref, fw_ref, o_ref):
    r = x_ref[...] + _dot(m_ref[...], w_ref[...])
    ms = jnp.mean(r * r, axis=-1, keepdims=True)
    o_ref[...] = r * lax.rsqrt(ms + EPS) * fw_ref[...]


def _output(mixed, w_out, x2, final_w):
    t = x2.shape[0]
    return pl.pallas_call(
        _out_kernel,
        grid=(t // OUT_TM,),
        in_specs=[
            pl.BlockSpec((OUT_TM, D_MODEL), lambda i: (i, 0)),
            pl.BlockSpec((D_MODEL, D_MODEL), lambda i: (0, 0)),
            pl.BlockSpec((OUT_TM, D_MODEL), lambda i: (i, 0)),
            pl.BlockSpec((1, D_MODEL), lambda i: (0, 0)),
        ],
        out_specs=pl.BlockSpec((OUT_TM, D_MODEL), lambda i: (i, 0)),
        out_shape=jax.ShapeDtypeStruct((t, D_MODEL), F32),
        compiler_params=pltpu.CompilerParams(
            dimension_semantics=("arbitrary",), vmem_limit_bytes=VMEM_LIMIT),
        name="output_norm",
    )(mixed, w_out, x2, final_w)


def _pad_lanes(v):
    return jnp.pad(v, (0, LANES - v.shape[0])).reshape(1, LANES)


def kernel(x, norm_w, w_in, conv_w, conv_b, dt_bias, A_log, D_skip, ssm_norm_w,
           w_ssm_proj, w_attn_proj, gate_bias, w_out, final_norm_w):
    batch, seq, _ = x.shape
    depth = norm_w.shape[0]
    assert depth == 1, "the block is built for a single layer"
    l = 0
    dt_lo = D_INNER + (D_INNER + 2 * N_GROUPS * N_STATE)
    dt_hi = dt_lo + N_HEADS

    head_of_channel = jnp.arange(D_INNER, dtype=jnp.int32) // HEAD_DIM
    expand = (jnp.arange(LANES, dtype=jnp.int32)[:, None] == head_of_channel[None, :]).astype(BF16)
    expand = expand.reshape(LANES, N_GROUPS, GROUP_W).transpose(1, 0, 2)
    rows = jnp.arange(CHUNK, dtype=jnp.int32)
    shifts = jnp.stack([(rows[:, None] - rows[None, :] == d).astype(BF16) for d in range(1, CONV_K)])

    x2 = x.reshape(batch * seq, D_MODEL)
    w = w_in[l]
    w_main = jnp.concatenate([w[:, :dt_lo], w[:, dt_hi:]], axis=1).astype(BF16)
    w_dt = jnp.pad(w[:, dt_lo:dt_hi], ((0, 0), (0, LANES - N_HEADS)))
    wdt_hi, wdt_lo = _split2(w_dt)

    proj, dt_raw = _in_projection(x2, norm_w[l].reshape(1, D_MODEL), w_main, wdt_hi, wdt_lo)

    yn = _ssd_branch(
        proj, dt_raw, conv_w[l], conv_b[l].reshape(1, -1),
        _pad_lanes(dt_bias[l]), _pad_lanes(A_log[l]),
        jnp.repeat(D_skip[l], HEAD_DIM).reshape(1, D_INNER),
        ssm_norm_w[l].reshape(1, D_INNER), expand, shifts, batch, seq)

    og = _moba_branch(proj, batch, seq)

    mixed = _gated_merge(yn, og, w_ssm_proj[l].astype(BF16), w_attn_proj[l].astype(BF16),
                         proj, gate_bias[l].reshape(1, -1))
    out = _output(mixed, w_out[l].astype(BF16), x2, final_norm_w.reshape(1, D_MODEL))
    return out.reshape(batch, seq, D_MODEL)
```

```python
import jax
import jax.numpy as jnp
from jax import lax
from jax.experimental import pallas as pl
from jax.experimental.pallas import tpu as pltpu

F32 = jnp.float32
BF16 = jnp.bfloat16

D_MODEL = 2048
D_INNER = 4096
HEAD_DIM = 64
N_HEADS = 64
N_GROUPS = 8
HEADS_PER_GROUP = N_HEADS // N_GROUPS
GROUP_W = D_INNER // N_GROUPS
N_STATE = 128
CONV_W = GROUP_W + 2 * N_STATE
CONV_K = 4
CHUNK = 256
HALF = CHUNK // 2
ATTN_HEADS = 16
ATTN_DH = 128
MOBA_BLOCK = 256
MOBA_TOPK = 3
EPS = 1e-6
NEG = -1e30
LOG2E = 1.4426950408889634

OFF_Z = 0
OFF_X = 4096
OFF_B = 8192
OFF_C = 9216
OFF_Q = 10240
OFF_K = 12288
OFF_V = 14336
OFF_GA = 16384
OFF_GM = 18432
PROJ_W = 22528

LANES = 128
SUBLANES = 8
BF16_ROWS = 16
VMEM_LIMIT = 56 * 1024 * 1024


def _dot(a, b):
    return jnp.dot(a, b, preferred_element_type=F32)


def _dot_nt(a, b):
    return lax.dot_general(a, b, (((1,), (1,)), ((), ())), preferred_element_type=F32)


def _dot_tn(a, b):
    return lax.dot_general(a, b, (((0,), (0,)), ((), ())), preferred_element_type=F32)


def _split2(a):
    hi = a.astype(BF16)
    lo = (a - hi.astype(F32)).astype(BF16)
    return hi, lo


def _split3(a):
    a1 = a.astype(BF16)
    r1 = a - a1.astype(F32)
    a2 = r1.astype(BF16)
    a3 = (r1 - a2.astype(F32)).astype(BF16)
    return a1, a2, a3


def _sigmoid(x):
    return 0.5 + 0.5 * jnp.tanh(0.5 * x)


def _silu(x):
    h = 0.5 * x
    return h + h * jnp.tanh(h)


IN_TM = 1024
IN_TN = 1024


IN_TILES_A = OFF_Q // IN_TN


def _inproj_kernel(x_ref, nw_ref, wa_ref, wb_ref, wdh_ref, wdl_ref, o_ref, dt_ref, h_ref):
    j = pl.program_id(1)

    @pl.when(j == 0)
    def _():
        x = x_ref[...]
        ms = jnp.mean(x * x, axis=-1, keepdims=True)
        h = x * lax.rsqrt(ms + EPS) * nw_ref[...]
        hb, hl = _split2(h)
        h_ref[...] = hb
        wdh = wdh_ref[...]
        dt_ref[...] = _dot(hb, wdh) + _dot(hb, wdl_ref[...]) + _dot(hl, wdh)

    @pl.when(j < IN_TILES_A)
    def _():
        o_ref[...] = _dot(h_ref[...], wa_ref[...]).astype(o_ref.dtype)

    @pl.when(j >= IN_TILES_A)
    def _():
        o_ref[...] = _dot(h_ref[...], wb_ref[...]).astype(o_ref.dtype)


def _in_projection(x2, norm_w, w_all, w_after_dt, wdt_hi, wdt_lo):
    t = x2.shape[0]
    grid = (t // IN_TM, PROJ_W // IN_TN)
    assert OFF_Q % IN_TN == 0
    return pl.pallas_call(
        _inproj_kernel,
        grid=grid,
        in_specs=[
            pl.BlockSpec((IN_TM, D_MODEL), lambda i, j: (i, 0)),
            pl.BlockSpec((1, D_MODEL), lambda i, j: (0, 0)),
            pl.BlockSpec((D_MODEL, IN_TN), lambda i, j: (0, jnp.minimum(j, IN_TILES_A - 1))),
            pl.BlockSpec((D_MODEL, IN_TN), lambda i, j: (0, jnp.maximum(j - IN_TILES_A, 0))),
            pl.BlockSpec((D_MODEL, LANES), lambda i, j: (0, 0)),
            pl.BlockSpec((D_MODEL, LANES), lambda i, j: (0, 0)),
        ],
        out_specs=[
            pl.BlockSpec((IN_TM, IN_TN), lambda i, j: (i, j)),
            pl.BlockSpec((IN_TM, LANES), lambda i, j: (i, 0)),
        ],
        out_shape=[
            jax.ShapeDtypeStruct((t, PROJ_W), BF16),
            jax.ShapeDtypeStruct((t, LANES), F32),
        ],
        scratch_shapes=[pltpu.VMEM((IN_TM, D_MODEL), BF16)],
        compiler_params=pltpu.CompilerParams(
            dimension_semantics=("arbitrary", "arbitrary"), vmem_limit_bytes=VMEM_LIMIT),
        name="in_projection",
    )(x2, norm_w, w_all, w_after_dt, wdt_hi, wdt_lo)


SSD_GROUPS_PER_TRIP = 2


def _ssd_kernel(z_ref, x_ref, bc_ref, dtraw_ref, cw_ref, cbias_ref,
                dtb_ref, alog_ref, dskip_ref, nw_ref, e_ref, shift_ref,
                y_ref,
                state_ref, tail_ref, halo_ref, cumg_ref, cumt_ref, hq_ref, dlast_ref):
    c = pl.program_id(1)

    row = lax.broadcasted_iota(jnp.int32, (CHUNK, CHUNK), 0)
    col = lax.broadcasted_iota(jnp.int32, (CHUNK, CHUNK), 1)
    causal = row >= col

    def chunk_quantities():
        dtr = dtraw_ref[...] + dtb_ref[...]
        dt = jnp.maximum(dtr, 0.0) + jnp.log1p(jnp.exp(-jnp.abs(dtr)))
        a = dt * (-jnp.exp(alog_ref[...]))
        a1, a2, a3 = _split3(a)
        tril = causal.astype(BF16)
        cum = _dot(tril, a1) + _dot(tril, a2) + _dot(tril, a3)
        cum2 = cum * LOG2E
        cumt_ref[...] = cum2.T
        for gg in range(N_GROUPS):
            sh = (LANES - HEADS_PER_GROUP * gg) % LANES
            cumg_ref[gg] = cum2 if sh == 0 else pltpu.roll(cum2, sh, axis=1)
        last = cum[CHUNK - 1:CHUNK, :]
        hq_ref[0] = dt.astype(BF16)
        hq_ref[1] = jnp.exp(cum).astype(BF16)
        hq_ref[2] = jnp.exp(last - cum).astype(BF16)
        dlast_ref[...] = jnp.broadcast_to(jnp.exp(last), (SUBLANES, LANES))

    chunk_quantities()

    @pl.when(c == 0)
    def _():
        tail_ref[...] = jnp.zeros(tail_ref.shape, F32)
        state_ref[...] = jnp.zeros(state_ref.shape, F32)

    causal_top = causal[0:HALF, 0:HALF]
    lane = lax.broadcasted_iota(jnp.int32, (CHUNK, LANES), 1)

    def group(g, carry):
        gx = pl.multiple_of(g * GROUP_W, GROUP_W)
        gb = pl.multiple_of(g * N_STATE, N_STATE)
        gc = pl.multiple_of(N_GROUPS * N_STATE + g * N_STATE, N_STATE)

        def conv_channels(ref):
            return jnp.concatenate([ref[:, pl.ds(gx, GROUP_W)],
                                    ref[:, pl.ds(pl.multiple_of(D_INNER + gb, N_STATE), N_STATE)],
                                    ref[:, pl.ds(pl.multiple_of(D_INNER + gc, N_STATE), N_STATE)]], axis=1)

        xin = jnp.concatenate([x_ref[:, pl.ds(gx, GROUP_W)], bc_ref[:, pl.ds(gb, N_STATE)],
                               bc_ref[:, pl.ds(gc, N_STATE)]], axis=1)
        xin_f = xin.astype(F32)
        cw = conv_channels(cw_ref)
        cbias = conv_channels(cbias_ref)
        halo_ref[0:SUBLANES, :] = tail_ref[g]
        halo_ref[SUBLANES:2 * SUBLANES, :] = jnp.zeros((SUBLANES, CONV_W), F32)
        tail_ref[g] = xin_f[CHUNK - SUBLANES:CHUNK, :]
        acc = cbias + cw[CONV_K - 1:CONV_K, :] * xin_f
        cor = jnp.zeros((SUBLANES, CONV_W), F32)
        for d in range(1, CONV_K):
            wk = cw[CONV_K - 1 - d:CONV_K - d, :]
            acc = acc + wk * _dot(shift_ref[d - 1], xin)
            cor = cor + wk * halo_ref[SUBLANES - d:2 * SUBLANES - d, :]
        acc = jnp.concatenate([acc[0:SUBLANES] + cor, acc[SUBLANES:]], axis=0)
        act = _silu(acc)
        xc = act[:, 0:GROUP_W]
        bmat = act[:, GROUP_W:GROUP_W + N_STATE].astype(BF16)
        cmat = act[:, GROUP_W + N_STATE:CONV_W].astype(BF16)

        eg = e_ref[g]
        e_dt = _dot(hq_ref[0], eg)
        e_cum = _dot(hq_ref[1], eg)
        e_end = _dot(hq_ref[2], eg)
        dh, dlo = _split2(dlast_ref[...])
        dl = _dot(dh, eg) + _dot(dlo, eg)

        cb = _dot_nt(cmat, bmat)
        cb_top = cb[0:HALF, 0:HALF]
        cb_bot = cb[HALF:CHUNK, :]
        xdt = xc * e_dt
        xdt_b = xdt.astype(BF16)
        xde_b = (xdt * e_end).astype(BF16)

        cg = cumg_ref[g]
        ys = []
        for pair in range(HEADS_PER_GROUP // 2):
            xp = xdt_b[:, LANES * pair:LANES * (pair + 1)]
            top = None
            bot = None
            for sub in range(2):
                hh = 2 * pair + sub
                colv = cg[:, hh:hh + 1]
                rowv = cumt_ref[pl.ds(g * HEADS_PER_GROUP + hh, 1), :]
                seg_t = jnp.where(causal_top, colv[0:HALF] - rowv[:, 0:HALF], NEG)
                seg_l = colv[HALF:CHUNK] - rowv[:, 0:HALF]
                seg_r = jnp.where(causal_top, colv[HALF:CHUNK] - rowv[:, HALF:CHUNK], NEG)
                w_t = (cb_top * jnp.exp2(seg_t)).astype(BF16)
                w_b = (cb_bot * jnp.exp2(jnp.concatenate([seg_l, seg_r], axis=1))).astype(BF16)
                in_head = (lane >= HEAD_DIM * sub) & (lane < HEAD_DIM * (sub + 1))
                rhs = jnp.where(in_head, xp, jnp.zeros_like(xp))
                t_t = _dot(w_t, rhs[0:HALF])
                t_b = _dot(w_b, rhs)
                top = t_t if top is None else top + t_t
                bot = t_b if bot is None else bot + t_b
            ys.append(jnp.concatenate([top, bot], axis=0))
        y = jnp.concatenate(ys, axis=1)

        st = state_ref[g]
        y = y + _dot(cmat, st.astype(BF16)) * e_cum
        y = y + dskip_ref[:, pl.ds(gx, GROUP_W)] * xc
        y = y * _silu(z_ref[:, pl.ds(gx, GROUP_W)].astype(F32))
        ms = jnp.mean(y * y, axis=-1, keepdims=True)
        y = y * lax.rsqrt(ms + EPS) * nw_ref[:, pl.ds(gx, GROUP_W)]
        y_ref[:, pl.ds(gx, GROUP_W)] = y.astype(y_ref.dtype)

        state_ref[g] = st * dl[0:1, :] + _dot_tn(bmat, xde_b)
        return carry

    def group_run(t, carry):
        for v in range(SSD_GROUPS_PER_TRIP):
            group(SSD_GROUPS_PER_TRIP * t + v, carry)
        return carry

    lax.fori_loop(0, N_GROUPS // SSD_GROUPS_PER_TRIP, group_run, 0)


def _ssd_branch(proj, dt_raw, conv_w, conv_b, dtb_pad, alog_pad, dskip_exp, ssm_norm_w, expand, shifts,
                batch, seq):
    t = proj.shape[0]
    nc = seq // CHUNK

    bc_w = 2 * N_GROUPS * N_STATE
    conv_ch = D_INNER + bc_w

    def rows(b, c):
        return b * nc + c

    in_specs = [
        pl.BlockSpec((CHUNK, D_INNER), lambda b, c: (rows(b, c), OFF_Z // D_INNER)),
        pl.BlockSpec((CHUNK, D_INNER), lambda b, c: (rows(b, c), OFF_X // D_INNER)),
        pl.BlockSpec((CHUNK, bc_w), lambda b, c: (rows(b, c), OFF_B // bc_w)),
        pl.BlockSpec((CHUNK, LANES), lambda b, c: (rows(b, c), 0)),
        pl.BlockSpec((CONV_K, conv_ch), lambda b, c: (0, 0)),
        pl.BlockSpec((1, conv_ch), lambda b, c: (0, 0)),
        pl.BlockSpec((1, LANES), lambda b, c: (0, 0)),
        pl.BlockSpec((1, LANES), lambda b, c: (0, 0)),
        pl.BlockSpec((1, D_INNER), lambda b, c: (0, 0)),
        pl.BlockSpec((1, D_INNER), lambda b, c: (0, 0)),
        pl.BlockSpec((N_GROUPS, LANES, GROUP_W), lambda b, c: (0, 0, 0)),
        pl.BlockSpec((CONV_K - 1, CHUNK, CHUNK), lambda b, c: (0, 0, 0)),
    ]
    return pl.pallas_call(
        _ssd_kernel,
        grid=(batch, nc),
        in_specs=in_specs,
        out_specs=pl.BlockSpec((CHUNK, D_INNER), lambda b, c: (rows(b, c), 0)),
        out_shape=jax.ShapeDtypeStruct((t, D_INNER), BF16),
        scratch_shapes=[
            pltpu.VMEM((N_GROUPS, N_STATE, GROUP_W), F32),
            pltpu.VMEM((N_GROUPS, SUBLANES, CONV_W), F32),
            pltpu.VMEM((2 * SUBLANES, CONV_W), F32),
            pltpu.VMEM((N_GROUPS, CHUNK, LANES), F32),
            pltpu.VMEM((LANES, CHUNK), F32),
            pltpu.VMEM((3, CHUNK, LANES), BF16),
            pltpu.VMEM((SUBLANES, LANES), F32),
        ],
        compiler_params=pltpu.CompilerParams(
            dimension_semantics=("arbitrary", "arbitrary"), vmem_limit_bytes=VMEM_LIMIT),
        name="ssd_branch",
    )(proj, proj, proj, dt_raw, conv_w, conv_b, dtb_pad, alog_pad, dskip_exp, ssm_norm_w, expand, shifts)


HEADS_PER_STEP = 4
STEP_W = HEADS_PER_STEP * ATTN_DH
KV_PER_ITER = 2
GROUPS_PER_TRIP = 8
QBLOCKS_PER_STEP = 2
ACC_ROWS = ATTN_DH + BF16_ROWS
EXP2_SCALE = (ATTN_DH ** -0.5) * LOG2E


def _moba_kernel(q_ref, k_ref, v_ref, g_ref, o_ref, vt_ref, kmean_ref, *scratch):
    step = pl.program_id(2)
    nb = k_ref.shape[0] // MOBA_BLOCK

    @pl.when(step == 0)
    def _():
        def prep(jb, carry):
            off = pl.multiple_of(jb * MOBA_BLOCK, MOBA_BLOCK)
            vt = v_ref[pl.ds(off, MOBA_BLOCK), :].astype(F32).T.astype(BF16)
            for a in range(HEADS_PER_STEP):
                vt_ref[jb, a, 0:ATTN_DH, :] = vt[ATTN_DH * a:ATTN_DH * (a + 1), :]
                vt_ref[jb, a, ATTN_DH:ACC_ROWS, :] = jnp.ones((BF16_ROWS, MOBA_BLOCK), BF16)
            kb = k_ref[pl.ds(off, MOBA_BLOCK), :].astype(F32)
            kmean_ref[pl.ds(jb, 1), :] = jnp.mean(kb, axis=0, keepdims=True)
            return carry
        lax.fori_loop(0, nb, prep, 0)

    def one_block(sub, carry):
        _moba_qblock(QBLOCKS_PER_STEP * step + sub, pl.multiple_of(sub * MOBA_BLOCK, MOBA_BLOCK),
                     q_ref, k_ref, g_ref, o_ref, vt_ref, kmean_ref, *scratch)
        return carry
    lax.fori_loop(0, QBLOCKS_PER_STEP, one_block, 0)


def _moba_qblock(i, row0, q_ref, k_ref, g_ref, o_ref,
                 vt_ref, kmean_ref, bias_ref, sa_ref, mxa_ref, sb_ref, mxb_ref, acc_ref, m_ref):
    seq = k_ref.shape[0]
    nb = seq // MOBA_BLOCK
    n_groups = nb // KV_PER_ITER
    rows = pl.ds(row0, MOBA_BLOCK)

    km_hi, km_lo = _split2(kmean_ref[...])
    blk = lax.broadcasted_iota(jnp.int32, (nb, MOBA_BLOCK), 0)
    past = blk < i
    key = lax.broadcasted_iota(jnp.int32, (MOBA_BLOCK, MOBA_BLOCK), 0)
    qry = lax.broadcasted_iota(jnp.int32, (MOBA_BLOCK, MOBA_BLOCK), 1)
    own = pl.multiple_of(i * MOBA_BLOCK, MOBA_BLOCK)

    def scores(off, a, qa):
        kk = k_ref[pl.ds(off, MOBA_BLOCK), ATTN_DH * a:ATTN_DH * (a + 1)]
        return _dot_nt(jnp.concatenate([kk, kk], axis=1), qa)

    heads = range(HEADS_PER_STEP)
    qs = [q_ref[rows, ATTN_DH * a:ATTN_DH * (a + 1)] for a in heads]
    qas = []
    for a in heads:
        q_hi, q_lo = _split2(qs[a].astype(F32) * EXP2_SCALE)
        qas.append(jnp.concatenate([q_hi, q_lo], axis=1))

    gates = []
    for a in heads:
        lo, hi = ATTN_DH * a, ATTN_DH * (a + 1)
        gate = _dot_nt(km_hi[:, lo:hi], qs[a]) + _dot_nt(km_lo[:, lo:hi], qs[a])
        gates.append(jnp.where(past, gate, NEG))
    sels = [jnp.zeros((nb, MOBA_BLOCK), jnp.bool_) for a in heads]
    for _ in range(MOBA_TOPK):
        for a in heads:
            mx = jnp.max(gates[a], axis=0, keepdims=True)
            idx = jnp.min(jnp.where(gates[a] == mx, blk, 4 * nb), axis=0, keepdims=True)
            hit = blk == idx
            sels[a] = sels[a] | (hit & past)
            gates[a] = jnp.where(hit, -jnp.inf, gates[a])
    for a in heads:
        bias_ref[a] = jnp.where(sels[a], 0.0, NEG)

    for a in heads:
        s = jnp.where(key <= qry, scores(own, a, qas[a]), NEG)
        m = jnp.max(s, axis=0, keepdims=True)
        p = jnp.exp2(s - m)
        acc_ref[a] = _dot(vt_ref[i, a], p.astype(BF16))
        m_ref[a:a + 1, :] = m

    def score_tile(a, u, grp, s_ref, mx_ref):
        j = KV_PER_ITER * grp + u
        s = scores(pl.multiple_of(j * MOBA_BLOCK, MOBA_BLOCK), a, qas[a])
        s_ref[a, u] = s
        r = a * KV_PER_ITER + u
        mx_ref[r:r + 1, :] = jnp.max(s, axis=0, keepdims=True) + bias_ref[a, pl.ds(j, 1), :]

    def score_stage(a, grp, s_ref, mx_ref):
        for u in range(KV_PER_ITER):
            score_tile(a, u, grp, s_ref, mx_ref)

    def half_step(grp, cur, nxt):
        s_ref, mx_ref = cur
        nxt_grp = jnp.minimum(grp + 1, n_groups - 1)
        for a in range(HEADS_PER_STEP):
            m = m_ref[a:a + 1, :]
            mx = jnp.max(mx_ref[a * KV_PER_ITER:(a + 1) * KV_PER_ITER, :], axis=0, keepdims=True)
            m_new = jnp.maximum(m, mx)
            acc = jnp.exp2(m - m_new) * acc_ref[a]
            for u in range(KV_PER_ITER):
                j = KV_PER_ITER * grp + u
                m_eff = m_new - bias_ref[a, pl.ds(j, 1), :]
                p = jnp.exp2(s_ref[a, u] - m_eff)
                acc = acc + _dot(vt_ref[j, a], p.astype(BF16))
            acc_ref[a] = acc
            m_ref[a:a + 1, :] = m_new
            if nxt is not None:
                score_stage(a, nxt_grp, *nxt)

    buf_a = (sa_ref, mxa_ref)
    buf_b = (sb_ref, mxb_ref)

    def pair(grp):
        half_step(grp, buf_a, buf_b)
        half_step(grp + 1, buf_b, buf_a)

    def trip(tt, carry):
        for v in range(0, GROUPS_PER_TRIP, 2):
            pair(GROUPS_PER_TRIP * tt + v)
        return carry

    n_iter = (i + KV_PER_ITER - 1) // KV_PER_ITER
    for a in range(HEADS_PER_STEP):
        score_stage(a, 0, *buf_a)
    lax.fori_loop(0, n_iter // GROUPS_PER_TRIP, trip, 0)
    rest = n_iter % GROUPS_PER_TRIP

    w = GROUPS_PER_TRIP // 2
    while w >= 2:
        @pl.when((rest // w) % 2 == 1)
        def _(w=w):
            for v in range(0, w, 2):
                pair(n_iter - rest % (2 * w) + v)
        w //= 2

    @pl.when(rest % 2 == 1)
    def _():
        half_step(n_iter - 1, buf_a, None)

    for a in range(HEADS_PER_STEP):
        lo, hi = ATTN_DH * a, ATTN_DH * (a + 1)
        acc = acc_ref[a]
        denom = acc[ATTN_DH:ATTN_DH + 1, :]
        o = (acc[0:ATTN_DH, :] * (1.0 / denom)).T
        o_ref[rows, lo:hi] = (o * _silu(g_ref[rows, lo:hi].astype(F32))).astype(o_ref.dtype)


def _moba_branch(proj, batch, seq):
    t = proj.shape[0]
    nb = seq // MOBA_BLOCK
    assert nb % (2 * KV_PER_ITER) == 0 and nb % QBLOCKS_PER_STEP == 0
    steps = nb // QBLOCKS_PER_STEP
    q_rows = QBLOCKS_PER_STEP * MOBA_BLOCK

    def qrow(b, h, i):
        return b * steps + i

    score_buf = pltpu.VMEM((HEADS_PER_STEP, KV_PER_ITER, MOBA_BLOCK, MOBA_BLOCK), F32)
    max_buf = pltpu.VMEM((HEADS_PER_STEP * KV_PER_ITER, MOBA_BLOCK), F32)
    return pl.pallas_call(
        _moba_kernel,
        grid=(batch, ATTN_HEADS // HEADS_PER_STEP, steps),
        in_specs=[
            pl.BlockSpec((q_rows, STEP_W), lambda b, h, i: (qrow(b, h, i), OFF_Q // STEP_W + h)),
            pl.BlockSpec((seq, STEP_W), lambda b, h, i: (b, OFF_K // STEP_W + h)),
            pl.BlockSpec((seq, STEP_W), lambda b, h, i: (b, OFF_V // STEP_W + h)),
            pl.BlockSpec((q_rows, STEP_W), lambda b, h, i: (qrow(b, h, i), OFF_GA // STEP_W + h)),
        ],
        out_specs=pl.BlockSpec((q_rows, STEP_W), lambda b, h, i: (qrow(b, h, i), h)),
        out_shape=jax.ShapeDtypeStruct((t, ATTN_HEADS * ATTN_DH), BF16),
        scratch_shapes=[
            pltpu.VMEM((nb, HEADS_PER_STEP, ACC_ROWS, MOBA_BLOCK), BF16),
            pltpu.VMEM((nb, STEP_W), F32),
            pltpu.VMEM((HEADS_PER_STEP, nb, MOBA_BLOCK), F32),
            score_buf, max_buf, score_buf, max_buf,
            pltpu.VMEM((HEADS_PER_STEP, ACC_ROWS, MOBA_BLOCK), F32),
            pltpu.VMEM((SUBLANES, MOBA_BLOCK), F32),
        ],
        compiler_params=pltpu.CompilerParams(
            dimension_semantics=("arbitrary", "arbitrary", "arbitrary"), vmem_limit_bytes=VMEM_LIMIT),
        name="moba_branch",
    )(proj, proj, proj, proj)


MIX_TM = 1024
MIX_TN = 512
OUT_TM = 512


def _mix_kernel(yn_ref, og_ref, ws_ref, wa_ref, gs_ref, ga_ref, bs_ref, ba_ref, o_ref):
    y_ssm = _dot(yn_ref[...], ws_ref[...])
    y_att = _dot(og_ref[...], wa_ref[...])
    g_ssm = _sigmoid(gs_ref[...].astype(F32) + bs_ref[...])
    g_att = _sigmoid(ga_ref[...].astype(F32) + ba_ref[...])
    o_ref[...] = (g_ssm * y_ssm + g_att * y_att).astype(o_ref.dtype)


def _gated_merge(yn, og, w_ssm, w_att, proj, gate_bias):
    t = yn.shape[0]
    nj = D_MODEL // MIX_TN
    return pl.pallas_call(
        _mix_kernel,
        grid=(t // MIX_TM, nj),
        in_specs=[
            pl.BlockSpec((MIX_TM, D_INNER), lambda i, j: (i, 0)),
            pl.BlockSpec((MIX_TM, D_MODEL), lambda i, j: (i, 0)),
            pl.BlockSpec((D_INNER, MIX_TN), lambda i, j: (0, j)),
            pl.BlockSpec((D_MODEL, MIX_TN), lambda i, j: (0, j)),
            pl.BlockSpec((MIX_TM, MIX_TN), lambda i, j: (i, OFF_GM // MIX_TN + j)),
            pl.BlockSpec((MIX_TM, MIX_TN), lambda i, j: (i, OFF_GM // MIX_TN + nj + j)),
            pl.BlockSpec((1, MIX_TN), lambda i, j: (0, j)),
            pl.BlockSpec((1, MIX_TN), lambda i, j: (0, nj + j)),
        ],
        out_specs=pl.BlockSpec((MIX_TM, MIX_TN), lambda i, j: (i, j)),
        out_shape=jax.ShapeDtypeStruct((t, D_MODEL), BF16),
        compiler_params=pltpu.CompilerParams(
            dimension_semantics=("arbitrary", "arbitrary"), vmem_limit_bytes=VMEM_LIMIT),
        name="gated_merge",
    )(yn, og, w_ssm, w_att, proj, proj, gate_bias, gate_bias)


def _out_kernel(m_ref, w_ref, x_ref, fw_ref, o_ref):
    r = x_ref[...] + _dot(m_ref[...], w_ref[...])
    ms = jnp.mean(r * r, axis=-1, keepdims=True)
    o_ref[...] = r * lax.rsqrt(ms + EPS) * fw_ref[...]


def _output(mixed, w_out, x2, final_w):
    t = x2.shape[0]
    return pl.pallas_call(
        _out_kernel,
        grid=(t // OUT_TM,),
        in_specs=[
            pl.BlockSpec((OUT_TM, D_MODEL), lambda i: (i, 0)),
            pl.BlockSpec((D_MODEL, D_MODEL), lambda i: (0, 0)),
            pl.BlockSpec((OUT_TM, D_MODEL), lambda i: (i, 0)),
            pl.BlockSpec((1, D_MODEL), lambda i: (0, 0)),
        ],
        out_specs=pl.BlockSpec((OUT_TM, D_MODEL), lambda i: (i, 0)),
        out_shape=jax.ShapeDtypeStruct((t, D_MODEL), F32),
        compiler_params=pltpu.CompilerParams(
            dimension_semantics=("arbitrary",), vmem_limit_bytes=VMEM_LIMIT),
        name="output_norm",
    )(mixed, w_out, x2, final_w)


def _pad_lanes(v):
    return jnp.pad(v, (0, LANES - v.shape[0])).reshape(1, LANES)


def kernel(x, norm_w, w_in, conv_w, conv_b, dt_bias, A_log, D_skip, ssm_norm_w,
           w_ssm_proj, w_attn_proj, gate_bias, w_out, final_norm_w):
    batch, seq, _ = x.shape
    depth = norm_w.shape[0]
    assert depth == 1, "the block is built for a single layer"
    l = 0
    dt_lo = D_INNER + (D_INNER + 2 * N_GROUPS * N_STATE)
    dt_hi = dt_lo + N_HEADS

    head_of_channel = jnp.arange(D_INNER, dtype=jnp.int32) // HEAD_DIM
    expand = (jnp.arange(LANES, dtype=jnp.int32)[:, None] == head_of_channel[None, :]).astype(BF16)
    expand = expand.reshape(LANES, N_GROUPS, GROUP_W).transpose(1, 0, 2)
    rows = jnp.arange(CHUNK, dtype=jnp.int32)
    shifts = jnp.stack([(rows[:, None] - rows[None, :] == d).astype(BF16) for d in range(1, CONV_K)])

    x2 = x.reshape(batch * seq, D_MODEL)
    w = w_in[l]
    assert dt_lo == OFF_Q
    w_bf = w.astype(BF16)
    w_dt = jnp.pad(w[:, dt_lo:dt_hi], ((0, 0), (0, LANES - N_HEADS)))
    wdt_hi, wdt_lo = _split2(w_dt)

    proj, dt_raw = _in_projection(x2, norm_w[l].reshape(1, D_MODEL), w_bf, w_bf[:, dt_hi:], wdt_hi, wdt_lo)

    yn = _ssd_branch(
        proj, dt_raw, conv_w[l], conv_b[l].reshape(1, -1),
        _pad_lanes(dt_bias[l]), _pad_lanes(A_log[l]),
        jnp.repeat(D_skip[l], HEAD_DIM).reshape(1, D_INNER),
        ssm_norm_w[l].reshape(1, D_INNER), expand, shifts, batch, seq)

    og = _moba_branch(proj, batch, seq)

    mixed = _gated_merge(yn, og, w_ssm_proj[l].astype(BF16), w_attn_proj[l].astype(BF16),
                         proj, gate_bias[l].reshape(1, -1))
    out = _output(mixed, w_out[l].astype(BF16), x2, final_norm_w.reshape(1, D_MODEL))
    return out.reshape(batch, seq, D_MODEL)
```

```python
import jax
import jax.numpy as jnp
from jax import lax
from jax.experimental import pallas as pl
from jax.experimental.pallas import tpu as pltpu

F32 = jnp.float32
BF16 = jnp.bfloat16

D_MODEL = 2048
D_INNER = 4096
HEAD_DIM = 64
N_HEADS = 64
N_GROUPS = 8
HEADS_PER_GROUP = N_HEADS // N_GROUPS
GROUP_W = D_INNER // N_GROUPS
N_STATE = 128
CONV_W = GROUP_W + 2 * N_STATE
CONV_K = 4
CHUNK = 256
HALF = CHUNK // 2
ATTN_HEADS = 16
ATTN_DH = 128
MOBA_BLOCK = 256
MOBA_TOPK = 3
EPS = 1e-6
NEG = -1e30
LOG2E = 1.4426950408889634

OFF_Z = 0
OFF_X = 4096
OFF_B = 8192
OFF_C = 9216
OFF_Q = 10240
OFF_K = 12288
OFF_V = 14336
OFF_GA = 16384
OFF_GM = 18432
PROJ_W = 22528

LANES = 128
SUBLANES = 8
BF16_ROWS = 16
VMEM_LIMIT = 56 * 1024 * 1024


def _dot(a, b):
    return jnp.dot(a, b, preferred_element_type=F32)


def _dot_nt(a, b):
    return lax.dot_general(a, b, (((1,), (1,)), ((), ())), preferred_element_type=F32)


def _dot_tn(a, b):
    return lax.dot_general(a, b, (((0,), (0,)), ((), ())), preferred_element_type=F32)


def _split2(a):
    hi = a.astype(BF16)
    lo = (a - hi.astype(F32)).astype(BF16)
    return hi, lo


def _split3(a):
    a1 = a.astype(BF16)
    r1 = a - a1.astype(F32)
    a2 = r1.astype(BF16)
    a3 = (r1 - a2.astype(F32)).astype(BF16)
    return a1, a2, a3


def _sigmoid(x):
    return 0.5 + 0.5 * jnp.tanh(0.5 * x)


def _silu(x):
    h = 0.5 * x
    return h + h * jnp.tanh(h)


NORM_TM = 512
IN_TM = 1024
IN_TN = 2048


def _norm_kernel(x_ref, nw_ref, wdh_ref, wdl_ref, h_ref, dt_ref):
    x = x_ref[...]
    ms = jnp.mean(x * x, axis=-1, keepdims=True)
    h = x * lax.rsqrt(ms + EPS) * nw_ref[...]
    hb, hl = _split2(h)
    h_ref[...] = hb
    wdh = wdh_ref[...]
    dt_ref[...] = _dot(hb, wdh) + _dot(hb, wdl_ref[...]) + _dot(hl, wdh)


def _inproj_kernel(h_ref, w_ref, o_ref):
    o_ref[...] = _dot(h_ref[...], w_ref[...]).astype(o_ref.dtype)


def _in_projection(x2, norm_w, w_main, wdt_hi, wdt_lo):
    t = x2.shape[0]
    h, dt_raw = pl.pallas_call(
        _norm_kernel,
        grid=(t // NORM_TM,),
        in_specs=[
            pl.BlockSpec((NORM_TM, D_MODEL), lambda i: (i, 0)),
            pl.BlockSpec((1, D_MODEL), lambda i: (0, 0)),
            pl.BlockSpec((D_MODEL, LANES), lambda i: (0, 0)),
            pl.BlockSpec((D_MODEL, LANES), lambda i: (0, 0)),
        ],
        out_specs=[
            pl.BlockSpec((NORM_TM, D_MODEL), lambda i: (i, 0)),
            pl.BlockSpec((NORM_TM, LANES), lambda i: (i, 0)),
        ],
        out_shape=[
            jax.ShapeDtypeStruct((t, D_MODEL), BF16),
            jax.ShapeDtypeStruct((t, LANES), F32),
        ],
        compiler_params=pltpu.CompilerParams(
            dimension_semantics=("arbitrary",), vmem_limit_bytes=VMEM_LIMIT),
        name="input_norm",
    )(x2, norm_w, wdt_hi, wdt_lo)
    proj = pl.pallas_call(
        _inproj_kernel,
        grid=(t // IN_TM, PROJ_W // IN_TN),
        in_specs=[
            pl.BlockSpec((IN_TM, D_MODEL), lambda i, j: (i, 0)),
            pl.BlockSpec((D_MODEL, IN_TN), lambda i, j: (0, j)),
        ],
        out_specs=pl.BlockSpec((IN_TM, IN_TN), lambda i, j: (i, j)),
        out_shape=jax.ShapeDtypeStruct((t, PROJ_W), BF16),
        compiler_params=pltpu.CompilerParams(
            dimension_semantics=("arbitrary", "arbitrary"), vmem_limit_bytes=VMEM_LIMIT),
        name="in_projection",
    )(h, w_main)
    return proj, dt_raw


SSD_GROUPS_PER_TRIP = 2


def _ssd_kernel(z_ref, x_ref, bc_ref, dtraw_ref, cw_ref, cbias_ref,
                dtb_ref, alog_ref, dskip_ref, nw_ref, e_ref, shift_ref,
                y_ref,
                state_ref, tail_ref, halo_ref, cumg_ref, cumt_ref, hq_ref, dlast_ref):
    c = pl.program_id(1)

    row = lax.broadcasted_iota(jnp.int32, (CHUNK, CHUNK), 0)
    col = lax.broadcasted_iota(jnp.int32, (CHUNK, CHUNK), 1)
    causal = row >= col

    def chunk_quantities():
        dtr = dtraw_ref[...] + dtb_ref[...]
        dt = jnp.maximum(dtr, 0.0) + jnp.log1p(jnp.exp(-jnp.abs(dtr)))
        a = dt * (-jnp.exp(alog_ref[...]))
        a1, a2, a3 = _split3(a)
        tril = causal.astype(BF16)
        cum = _dot(tril, a1) + _dot(tril, a2) + _dot(tril, a3)
        cum2 = cum * LOG2E
        cumt_ref[...] = cum2.T
        for gg in range(N_GROUPS):
            sh = (LANES - HEADS_PER_GROUP * gg) % LANES
            cumg_ref[gg] = cum2 if sh == 0 else pltpu.roll(cum2, sh, axis=1)
        last = cum[CHUNK - 1:CHUNK, :]
        hq_ref[0] = dt.astype(BF16)
        hq_ref[1] = jnp.exp(cum).astype(BF16)
        hq_ref[2] = jnp.exp(last - cum).astype(BF16)
        dlast_ref[...] = jnp.broadcast_to(jnp.exp(last), (SUBLANES, LANES))

    chunk_quantities()

    @pl.when(c == 0)
    def _():
        tail_ref[...] = jnp.zeros(tail_ref.shape, F32)
        state_ref[...] = jnp.zeros(state_ref.shape, F32)

    causal_top = causal[0:HALF, 0:HALF]
    lane = lax.broadcasted_iota(jnp.int32, (CHUNK, LANES), 1)

    def group(g, carry):
        gx = pl.multiple_of(g * GROUP_W, GROUP_W)
        gb = pl.multiple_of(g * N_STATE, N_STATE)
        gc = pl.multiple_of(N_GROUPS * N_STATE + g * N_STATE, N_STATE)

        def conv_channels(ref):
            return jnp.concatenate([ref[:, pl.ds(gx, GROUP_W)],
                                    ref[:, pl.ds(pl.multiple_of(D_INNER + gb, N_STATE), N_STATE)],
                                    ref[:, pl.ds(pl.multiple_of(D_INNER + gc, N_STATE), N_STATE)]], axis=1)

        xin = jnp.concatenate([x_ref[:, pl.ds(gx, GROUP_W)], bc_ref[:, pl.ds(gb, N_STATE)],
                               bc_ref[:, pl.ds(gc, N_STATE)]], axis=1)
        xin_f = xin.astype(F32)
        cw = conv_channels(cw_ref)
        cbias = conv_channels(cbias_ref)
        halo_ref[0:SUBLANES, :] = tail_ref[g]
        halo_ref[SUBLANES:2 * SUBLANES, :] = jnp.zeros((SUBLANES, CONV_W), F32)
        tail_ref[g] = xin_f[CHUNK - SUBLANES:CHUNK, :]
        acc = cbias + cw[CONV_K - 1:CONV_K, :] * xin_f
        cor = jnp.zeros((SUBLANES, CONV_W), F32)
        for d in range(1, CONV_K):
            wk = cw[CONV_K - 1 - d:CONV_K - d, :]
            acc = acc + wk * _dot(shift_ref[d - 1], xin)
            cor = cor + wk * halo_ref[SUBLANES - d:2 * SUBLANES - d, :]
        acc = jnp.concatenate([acc[0:SUBLANES] + cor, acc[SUBLANES:]], axis=0)
        act = _silu(acc)
        xc = act[:, 0:GROUP_W]
        bmat = act[:, GROUP_W:GROUP_W + N_STATE].astype(BF16)
        cmat = act[:, GROUP_W + N_STATE:CONV_W].astype(BF16)

        eg = e_ref[g]
        e_dt = _dot(hq_ref[0], eg)
        e_cum = _dot(hq_ref[1], eg)
        e_end = _dot(hq_ref[2], eg)
        dh, dlo = _split2(dlast_ref[...])
        dl = _dot(dh, eg) + _dot(dlo, eg)

        cb = _dot_nt(cmat, bmat)
        cb_top = cb[0:HALF, 0:HALF]
        cb_bot = cb[HALF:CHUNK, :]
        xdt = xc * e_dt
        xdt_b = xdt.astype(BF16)
        xde_b = (xdt * e_end).astype(BF16)

        cg = cumg_ref[g]
        ys = []
        for pair in range(HEADS_PER_GROUP // 2):
            xp = xdt_b[:, LANES * pair:LANES * (pair + 1)]
            top = None
            bot = None
            for sub in range(2):
                hh = 2 * pair + sub
                colv = cg[:, hh:hh + 1]
                rowv = cumt_ref[pl.ds(g * HEADS_PER_GROUP + hh, 1), :]
                seg_t = jnp.where(causal_top, colv[0:HALF] - rowv[:, 0:HALF], NEG)
                seg_l = colv[HALF:CHUNK] - rowv[:, 0:HALF]
                seg_r = jnp.where(causal_top, colv[HALF:CHUNK] - rowv[:, HALF:CHUNK], NEG)
                w_t = (cb_top * jnp.exp2(seg_t)).astype(BF16)
                w_b = (cb_bot * jnp.exp2(jnp.concatenate([seg_l, seg_r], axis=1))).astype(BF16)
                in_head = (lane >= HEAD_DIM * sub) & (lane < HEAD_DIM * (sub + 1))
                rhs = jnp.where(in_head, xp, jnp.zeros_like(xp))
                t_t = _dot(w_t, rhs[0:HALF])
                t_b = _dot(w_b, rhs)
                top = t_t if top is None else top + t_t
                bot = t_b if bot is None else bot + t_b
            ys.append(jnp.concatenate([top, bot], axis=0))
        y = jnp.concatenate(ys, axis=1)

        st = state_ref[g]
        y = y + _dot(cmat, st.astype(BF16)) * e_cum
        y = y + dskip_ref[:, pl.ds(gx, GROUP_W)] * xc
        y = y * _silu(z_ref[:, pl.ds(gx, GROUP_W)].astype(F32))
        ms = jnp.mean(y * y, axis=-1, keepdims=True)
        y = y * lax.rsqrt(ms + EPS) * nw_ref[:, pl.ds(gx, GROUP_W)]
        y_ref[:, pl.ds(gx, GROUP_W)] = y.astype(y_ref.dtype)

        state_ref[g] = st * dl[0:1, :] + _dot_tn(bmat, xde_b)
        return carry

    def group_run(t, carry):
        for v in range(SSD_GROUPS_PER_TRIP):
            group(SSD_GROUPS_PER_TRIP * t + v, carry)
        return carry

    lax.fori_loop(0, N_GROUPS // SSD_GROUPS_PER_TRIP, group_run, 0)


def _ssd_branch(proj, dt_raw, conv_w, conv_b, dtb_pad, alog_pad, dskip_exp, ssm_norm_w, expand, shifts,
                batch, seq):
    t = proj.shape[0]
    nc = seq // CHUNK

    bc_w = 2 * N_GROUPS * N_STATE
    conv_ch = D_INNER + bc_w

    def rows(b, c):
        return b * nc + c

    in_specs = [
        pl.BlockSpec((CHUNK, D_INNER), lambda b, c: (rows(b, c), OFF_Z // D_INNER)),
        pl.BlockSpec((CHUNK, D_INNER), lambda b, c: (rows(b, c), OFF_X // D_INNER)),
        pl.BlockSpec((CHUNK, bc_w), lambda b, c: (rows(b, c), OFF_B // bc_w)),
        pl.BlockSpec((CHUNK, LANES), lambda b, c: (rows(b, c), 0)),
        pl.BlockSpec((CONV_K, conv_ch), lambda b, c: (0, 0)),
        pl.BlockSpec((1, conv_ch), lambda b, c: (0, 0)),
        pl.BlockSpec((1, LANES), lambda b, c: (0, 0)),
        pl.BlockSpec((1, LANES), lambda b, c: (0, 0)),
        pl.BlockSpec((1, D_INNER), lambda b, c: (0, 0)),
        pl.BlockSpec((1, D_INNER), lambda b, c: (0, 0)),
        pl.BlockSpec((N_GROUPS, LANES, GROUP_W), lambda b, c: (0, 0, 0)),
        pl.BlockSpec((CONV_K - 1, CHUNK, CHUNK), lambda b, c: (0, 0, 0)),
    ]
    return pl.pallas_call(
        _ssd_kernel,
        grid=(batch, nc),
        in_specs=in_specs,
        out_specs=pl.BlockSpec((CHUNK, D_INNER), lambda b, c: (rows(b, c), 0)),
        out_shape=jax.ShapeDtypeStruct((t, D_INNER), BF16),
        scratch_shapes=[
            pltpu.VMEM((N_GROUPS, N_STATE, GROUP_W), F32),
            pltpu.VMEM((N_GROUPS, SUBLANES, CONV_W), F32),
            pltpu.VMEM((2 * SUBLANES, CONV_W), F32),
            pltpu.VMEM((N_GROUPS, CHUNK, LANES), F32),
            pltpu.VMEM((LANES, CHUNK), F32),
            pltpu.VMEM((3, CHUNK, LANES), BF16),
            pltpu.VMEM((SUBLANES, LANES), F32),
        ],
        compiler_params=pltpu.CompilerParams(
            dimension_semantics=("arbitrary", "arbitrary"), vmem_limit_bytes=VMEM_LIMIT),
        name="ssd_branch",
    )(proj, proj, proj, dt_raw, conv_w, conv_b, dtb_pad, alog_pad, dskip_exp, ssm_norm_w, expand, shifts)


HEADS_PER_STEP = 4
STEP_W = HEADS_PER_STEP * ATTN_DH
KV_PER_ITER = 2
GROUPS_PER_TRIP = 8
ACC_ROWS = ATTN_DH + BF16_ROWS
EXP2_SCALE = (ATTN_DH ** -0.5) * LOG2E


def _moba_kernel(q_ref, k_ref, v_ref, g_ref, o_ref,
                 vt_ref, kmean_ref, bias_ref, sa_ref, mxa_ref, sb_ref, mxb_ref, acc_ref, m_ref):
    i = pl.program_id(2)
    seq = k_ref.shape[0]
    nb = seq // MOBA_BLOCK
    n_groups = nb // KV_PER_ITER

    @pl.when(i == 0)
    def _():
        def prep(jb, carry):
            off = pl.multiple_of(jb * MOBA_BLOCK, MOBA_BLOCK)
            vt = v_ref[pl.ds(off, MOBA_BLOCK), :].astype(F32).T.astype(BF16)
            for a in range(HEADS_PER_STEP):
                vt_ref[jb, a, 0:ATTN_DH, :] = vt[ATTN_DH * a:ATTN_DH * (a + 1), :]
                vt_ref[jb, a, ATTN_DH:ACC_ROWS, :] = jnp.ones((BF16_ROWS, MOBA_BLOCK), BF16)
            kb = k_ref[pl.ds(off, MOBA_BLOCK), :].astype(F32)
            kmean_ref[pl.ds(jb, 1), :] = jnp.mean(kb, axis=0, keepdims=True)
            return carry
        lax.fori_loop(0, nb, prep, 0)

    km_hi, km_lo = _split2(kmean_ref[...])
    blk = lax.broadcasted_iota(jnp.int32, (nb, MOBA_BLOCK), 0)
    past = blk < i
    key = lax.broadcasted_iota(jnp.int32, (MOBA_BLOCK, MOBA_BLOCK), 0)
    qry = lax.broadcasted_iota(jnp.int32, (MOBA_BLOCK, MOBA_BLOCK), 1)
    own = pl.multiple_of(i * MOBA_BLOCK, MOBA_BLOCK)

    def scores(off, a, qa):
        kk = k_ref[pl.ds(off, MOBA_BLOCK), ATTN_DH * a:ATTN_DH * (a + 1)]
        return _dot_nt(jnp.concatenate([kk, kk], axis=1), qa)

    heads = range(HEADS_PER_STEP)
    qs = [q_ref[:, ATTN_DH * a:ATTN_DH * (a + 1)] for a in heads]
    qas = []
    for a in heads:
        q_hi, q_lo = _split2(qs[a].astype(F32) * EXP2_SCALE)
        qas.append(jnp.concatenate([q_hi, q_lo], axis=1))

    gates = []
    for a in heads:
        lo, hi = ATTN_DH * a, ATTN_DH * (a + 1)
        gate = _dot_nt(km_hi[:, lo:hi], qs[a]) + _dot_nt(km_lo[:, lo:hi], qs[a])
        gates.append(jnp.where(past, gate, NEG))
    sels = [jnp.zeros((nb, MOBA_BLOCK), jnp.bool_) for a in heads]
    for _ in range(MOBA_TOPK):
        for a in heads:
            mx = jnp.max(gates[a], axis=0, keepdims=True)
            idx = jnp.min(jnp.where(gates[a] == mx, blk, 4 * nb), axis=0, keepdims=True)
            hit = blk == idx
            sels[a] = sels[a] | (hit & past)
            gates[a] = jnp.where(hit, -jnp.inf, gates[a])
    for a in heads:
        bias_ref[a] = jnp.where(sels[a], 0.0, NEG)

    for a in heads:
        s = jnp.where(key <= qry, scores(own, a, qas[a]), NEG)
        m = jnp.max(s, axis=0, keepdims=True)
        p = jnp.exp2(s - m)
        acc_ref[a] = _dot(vt_ref[i, a], p.astype(BF16))
        m_ref[a:a + 1, :] = m

    def score_tile(a, u, grp, s_ref, mx_ref):
        j = KV_PER_ITER * grp + u
        s = scores(pl.multiple_of(j * MOBA_BLOCK, MOBA_BLOCK), a, qas[a])
        s_ref[a, u] = s
        r = a * KV_PER_ITER + u
        mx_ref[r:r + 1, :] = jnp.max(s, axis=0, keepdims=True) + bias_ref[a, pl.ds(j, 1), :]

    def score_stage(a, grp, s_ref, mx_ref):
        for u in range(KV_PER_ITER):
            score_tile(a, u, grp, s_ref, mx_ref)

    def half_step(grp, cur, nxt):
        s_ref, mx_ref = cur
        nxt_grp = jnp.minimum(grp + 1, n_groups - 1)
        for a in range(HEADS_PER_STEP):
            m = m_ref[a:a + 1, :]
            mx = jnp.max(mx_ref[a * KV_PER_ITER:(a + 1) * KV_PER_ITER, :], axis=0, keepdims=True)
            m_new = jnp.maximum(m, mx)
            acc = jnp.exp2(m - m_new) * acc_ref[a]
            for u in range(KV_PER_ITER):
                j = KV_PER_ITER * grp + u
                m_eff = m_new - bias_ref[a, pl.ds(j, 1), :]
                p = jnp.exp2(s_ref[a, u] - m_eff)
                acc = acc + _dot(vt_ref[j, a], p.astype(BF16))
            acc_ref[a] = acc
            m_ref[a:a + 1, :] = m_new
            if nxt is not None:
                score_stage(a, nxt_grp, *nxt)

    buf_a = (sa_ref, mxa_ref)
    buf_b = (sb_ref, mxb_ref)

    def pair(grp):
        half_step(grp, buf_a, buf_b)
        half_step(grp + 1, buf_b, buf_a)

    def trip(tt, carry):
        for v in range(0, GROUPS_PER_TRIP, 2):
            pair(GROUPS_PER_TRIP * tt + v)
        return carry

    n_iter = (i + KV_PER_ITER - 1) // KV_PER_ITER
    for a in range(HEADS_PER_STEP):
        score_stage(a, 0, *buf_a)
    lax.fori_loop(0, n_iter // GROUPS_PER_TRIP, trip, 0)
    rest = n_iter % GROUPS_PER_TRIP

    w = GROUPS_PER_TRIP // 2
    while w >= 2:
        @pl.when((rest // w) % 2 == 1)
        def _(w=w):
            for v in range(0, w, 2):
                pair(n_iter - rest % (2 * w) + v)
        w //= 2

    @pl.when(rest % 2 == 1)
    def _():
        half_step(n_iter - 1, buf_a, None)

    for a in range(HEADS_PER_STEP):
        lo, hi = ATTN_DH * a, ATTN_DH * (a + 1)
        acc = acc_ref[a]
        denom = acc[ATTN_DH:ATTN_DH + 1, :]
        o = (acc[0:ATTN_DH, :] * (1.0 / denom)).T
        o_ref[:, lo:hi] = (o * _silu(g_ref[:, lo:hi].astype(F32))).astype(o_ref.dtype)


def _moba_branch(proj, batch, seq):
    t = proj.shape[0]
    nb = seq // MOBA_BLOCK
    assert nb % (2 * KV_PER_ITER) == 0

    def qrow(b, h, i):
        return b * nb + i

    score_buf = pltpu.VMEM((HEADS_PER_STEP, KV_PER_ITER, MOBA_BLOCK, MOBA_BLOCK), F32)
    max_buf = pltpu.VMEM((HEADS_PER_STEP * KV_PER_ITER, MOBA_BLOCK), F32)
    return pl.pallas_call(
        _moba_kernel,
        grid=(batch, ATTN_HEADS // HEADS_PER_STEP, nb),
        in_specs=[
            pl.BlockSpec((MOBA_BLOCK, STEP_W), lambda b, h, i: (qrow(b, h, i), OFF_Q // STEP_W + h)),
            pl.BlockSpec((seq, STEP_W), lambda b, h, i: (b, OFF_K // STEP_W + h)),
            pl.BlockSpec((seq, STEP_W), lambda b, h, i: (b, OFF_V // STEP_W + h)),
            pl.BlockSpec((MOBA_BLOCK, STEP_W), lambda b, h, i: (qrow(b, h, i), OFF_GA // STEP_W + h)),
        ],
        out_specs=pl.BlockSpec((MOBA_BLOCK, STEP_W), lambda b, h, i: (qrow(b, h, i), h)),
        out_shape=jax.ShapeDtypeStruct((t, ATTN_HEADS * ATTN_DH), BF16),
        scratch_shapes=[
            pltpu.VMEM((nb, HEADS_PER_STEP, ACC_ROWS, MOBA_BLOCK), BF16),
            pltpu.VMEM((nb, STEP_W), F32),
            pltpu.VMEM((HEADS_PER_STEP, nb, MOBA_BLOCK), F32),
            score_buf, max_buf, score_buf, max_buf,
            pltpu.VMEM((HEADS_PER_STEP, ACC_ROWS, MOBA_BLOCK), F32),
            pltpu.VMEM((SUBLANES, MOBA_BLOCK), F32),
        ],
        compiler_params=pltpu.CompilerParams(
            dimension_semantics=("arbitrary", "arbitrary", "arbitrary"), vmem_limit_bytes=VMEM_LIMIT),
        name="moba_branch",
    )(proj, proj, proj, proj)


MIX_TM = 1024
MIX_TN = 512
OUT_TM = 512


def _mix_kernel(yn_ref, og_ref, ws_ref, wa_ref, gs_ref, ga_ref, bs_ref, ba_ref, o_ref):
    y_ssm = _dot(yn_ref[...], ws_ref[...])
    y_att = _dot(og_ref[...], wa_ref[...])
    g_ssm = _sigmoid(gs_ref[...].astype(F32) + bs_ref[...])
    g_att = _sigmoid(ga_ref[...].astype(F32) + ba_ref[...])
    o_ref[...] = (g_ssm * y_ssm + g_att * y_att).astype(o_ref.dtype)


def _gated_merge(yn, og, w_ssm, w_att, proj, gate_bias):
    t = yn.shape[0]
    nj = D_MODEL // MIX_TN
    return pl.pallas_call(
        _mix_kernel,
        grid=(t // MIX_TM, nj),
        in_specs=[
            pl.BlockSpec((MIX_TM, D_INNER), lambda i, j: (i, 0)),
            pl.BlockSpec((MIX_TM, D_MODEL), lambda i, j: (i, 0)),
            pl.BlockSpec((D_INNER, MIX_TN), lambda i, j: (0, j)),
            pl.BlockSpec((D_MODEL, MIX_TN), lambda i, j: (0, j)),
            pl.BlockSpec((MIX_TM, MIX_TN), lambda i, j: (i, OFF_GM // MIX_TN + j)),
            pl.BlockSpec((MIX_TM, MIX_TN), lambda i, j: (i, OFF_GM // MIX_TN + nj + j)),
            pl.BlockSpec((1, MIX_TN), lambda i, j: (0, j)),
            pl.BlockSpec((1, MIX_TN), lambda i, j: (0, nj + j)),
        ],
        out_specs=pl.BlockSpec((MIX_TM, MIX_TN), lambda i, j: (i, j)),
        out_shape=jax.ShapeDtypeStruct((t, D_MODEL), BF16),
        compiler_params=pltpu.CompilerParams(
            dimension_semantics=("arbitrary", "arbitrary"), vmem_limit_bytes=VMEM_LIMIT),
        name="gated_merge",
    )(yn, og, w_ssm, w_att, proj, proj, gate_bias, gate_bias)


def _out_kernel(m_ref, w_ref, x_ref, fw_ref, o_ref):
    r = x_ref[...] + _dot(m_ref[...], w_ref[...])
    ms = jnp.mean(r * r, axis=-1, keepdims=True)
    o_ref[...] = r * lax.rsqrt(ms + EPS) * fw_ref[...]


def _output(mixed, w_out, x2, final_w):
    t = x2.shape[0]
    return pl.pallas_call(
        _out_kernel,
        grid=(t // OUT_TM,),
        in_specs=[
            pl.BlockSpec((OUT_TM, D_MODEL), lambda i: (i, 0)),
            pl.BlockSpec((D_MODEL, D_MODEL), lambda i: (0, 0)),
            pl.BlockSpec((OUT_TM, D_MODEL), lambda i: (i, 0)),
            pl.BlockSpec((1, D_MODEL), lambda i: (0, 0)),
        ],
        out_specs=pl.BlockSpec((OUT_TM, D_MODEL), lambda i: (i, 0)),
        out_shape=jax.ShapeDtypeStruct((t, D_MODEL), F32),
        compiler_params=pltpu.CompilerParams(
            dimension_semantics=("arbitrary",), vmem_limit_bytes=VMEM_LIMIT),
        name="output_norm",
    )(mixed, w_out, x2, final_w)


def _pad_lanes(v):
    return jnp.pad(v, (0, LANES - v.shape[0])).reshape(1, LANES)


def kernel(x, norm_w, w_in, conv_w, conv_b, dt_bias, A_log, D_skip, ssm_norm_w,
           w_ssm_proj, w_attn_proj, gate_bias, w_out, final_norm_w):
    batch, seq, _ = x.shape
    depth = norm_w.shape[0]
    assert depth == 1, "the block is built for a single layer"
    l = 0
    dt_lo = D_INNER + (D_INNER + 2 * N_GROUPS * N_STATE)
    dt_hi = dt_lo + N_HEADS

    head_of_channel = jnp.arange(D_INNER, dtype=jnp.int32) // HEAD_DIM
    expand = (jnp.arange(LANES, dtype=jnp.int32)[:, None] == head_of_channel[None, :]).astype(BF16)
    expand = expand.reshape(LANES, N_GROUPS, GROUP_W).transpose(1, 0, 2)
    rows = jnp.arange(CHUNK, dtype=jnp.int32)
    shifts = jnp.stack([(rows[:, None] - rows[None, :] == d).astype(BF16) for d in range(1, CONV_K)])

    x2 = x.reshape(batch * seq, D_MODEL)
    w = w_in[l]
    w_main = jnp.concatenate([w[:, :dt_lo], w[:, dt_hi:]], axis=1).astype(BF16)
    w_dt = jnp.pad(w[:, dt_lo:dt_hi], ((0, 0), (0, LANES - N_HEADS)))
    wdt_hi, wdt_lo = _split2(w_dt)

    proj, dt_raw = _in_projection(x2, norm_w[l].reshape(1, D_MODEL), w_main, wdt_hi, wdt_lo)

    yn = _ssd_branch(
        proj, dt_raw, conv_w[l], conv_b[l].reshape(1, -1),
        _pad_lanes(dt_bias[l]), _pad_lanes(A_log[l]),
        jnp.repeat(D_skip[l], HEAD_DIM).reshape(1, D_INNER),
        ssm_norm_w[l].reshape(1, D_INNER), expand, shifts, batch, seq)

    og = _moba_branch(proj, batch, seq)

    mixed = _gated_merge(yn, og, w_ssm_proj[l].astype(BF16), w_attn_proj[l].astype(BF16),
                         proj, gate_bias[l].reshape(1, -1))
    out = _output(mixed, w_out[l].astype(BF16), x2, final_norm_w.reshape(1, D_MODEL))
    return out.reshape(batch, seq, D_MODEL)
```

```python
import jax
import jax.numpy as jnp
from jax import lax
from jax.experimental import pallas as pl
from jax.experimental.pallas import tpu as pltpu

F32 = jnp.float32
BF16 = jnp.bfloat16

D_MODEL = 2048
D_INNER = 4096
HEAD_DIM = 64
N_HEADS = 64
N_GROUPS = 8
HEADS_PER_GROUP = N_HEADS // N_GROUPS
GROUP_W = D_INNER // N_GROUPS
N_STATE = 128
CONV_W = GROUP_W + 2 * N_STATE
CONV_K = 4
CHUNK = 256
HALF = CHUNK // 2
ATTN_HEADS = 16
ATTN_DH = 128
MOBA_BLOCK = 256
MOBA_TOPK = 3
EPS = 1e-6
NEG = -1e30
LOG2E = 1.4426950408889634

OFF_Z = 0
OFF_X = 4096
OFF_B = 8192
OFF_C = 9216
OFF_Q = 10240
OFF_K = 12288
OFF_V = 14336
OFF_GA = 16384
OFF_GM = 18432
PROJ_W = 22528

LANES = 128
SUBLANES = 8
BF16_ROWS = 16
VMEM_LIMIT = 56 * 1024 * 1024


def _dot(a, b):
    return jnp.dot(a, b, preferred_element_type=F32)


def _dot_nt(a, b):
    return lax.dot_general(a, b, (((1,), (1,)), ((), ())), preferred_element_type=F32)


def _dot_tn(a, b):
    return lax.dot_general(a, b, (((0,), (0,)), ((), ())), preferred_element_type=F32)


def _split2(a):
    hi = a.astype(BF16)
    lo = (a - hi.astype(F32)).astype(BF16)
    return hi, lo


def _split3(a):
    a1 = a.astype(BF16)
    r1 = a - a1.astype(F32)
    a2 = r1.astype(BF16)
    a3 = (r1 - a2.astype(F32)).astype(BF16)
    return a1, a2, a3


def _sigmoid(x):
    return 0.5 + 0.5 * jnp.tanh(0.5 * x)


def _silu(x):
    h = 0.5 * x
    return h + h * jnp.tanh(h)


NORM_TM = 512
IN_TM = 1024
IN_TN = 2048


def _norm_kernel(x_ref, nw_ref, wdh_ref, wdl_ref, h_ref, dt_ref):
    x = x_ref[...]
    ms = jnp.mean(x * x, axis=-1, keepdims=True)
    h = x * lax.rsqrt(ms + EPS) * nw_ref[...]
    hb, hl = _split2(h)
    h_ref[...] = hb
    wdh = wdh_ref[...]
    dt_ref[...] = _dot(hb, wdh) + _dot(hb, wdl_ref[...]) + _dot(hl, wdh)


def _inproj_kernel(h_ref, w_ref, o_ref):
    o_ref[...] = _dot(h_ref[...], w_ref[...]).astype(o_ref.dtype)


def _in_projection(x2, norm_w, w_main, wdt_hi, wdt_lo):
    t = x2.shape[0]
    h, dt_raw = pl.pallas_call(
        _norm_kernel,
        grid=(t // NORM_TM,),
        in_specs=[
            pl.BlockSpec((NORM_TM, D_MODEL), lambda i: (i, 0)),
            pl.BlockSpec((1, D_MODEL), lambda i: (0, 0)),
            pl.BlockSpec((D_MODEL, LANES), lambda i: (0, 0)),
            pl.BlockSpec((D_MODEL, LANES), lambda i: (0, 0)),
        ],
        out_specs=[
            pl.BlockSpec((NORM_TM, D_MODEL), lambda i: (i, 0)),
            pl.BlockSpec((NORM_TM, LANES), lambda i: (i, 0)),
        ],
        out_shape=[
            jax.ShapeDtypeStruct((t, D_MODEL), BF16),
            jax.ShapeDtypeStruct((t, LANES), F32),
        ],
        compiler_params=pltpu.CompilerParams(
            dimension_semantics=("arbitrary",), vmem_limit_bytes=VMEM_LIMIT),
        name="input_norm",
    )(x2, norm_w, wdt_hi, wdt_lo)
    proj = pl.pallas_call(
        _inproj_kernel,
        grid=(t // IN_TM, PROJ_W // IN_TN),
        in_specs=[
            pl.BlockSpec((IN_TM, D_MODEL), lambda i, j: (i, 0)),
            pl.BlockSpec((D_MODEL, IN_TN), lambda i, j: (0, j)),
        ],
        out_specs=pl.BlockSpec((IN_TM, IN_TN), lambda i, j: (i, j)),
        out_shape=jax.ShapeDtypeStruct((t, PROJ_W), BF16),
        compiler_params=pltpu.CompilerParams(
            dimension_semantics=("arbitrary", "arbitrary"), vmem_limit_bytes=VMEM_LIMIT),
        name="in_projection",
    )(h, w_main)
    return proj, dt_raw


SSD_GROUPS_PER_TRIP = 2


def _ssd_kernel(z_ref, x_ref, bc_ref, dtraw_ref, cw_ref, cbias_ref,
                dtb_ref, alog_ref, dskip_ref, nw_ref, e_ref, shift_ref,
                y_ref,
                state_ref, tail_ref, halo_ref, cumg_ref, cumt_ref, hq_ref, dlast_ref):
    c = pl.program_id(1)

    row = lax.broadcasted_iota(jnp.int32, (CHUNK, CHUNK), 0)
    col = lax.broadcasted_iota(jnp.int32, (CHUNK, CHUNK), 1)
    causal = row >= col

    def chunk_quantities():
        dtr = dtraw_ref[...] + dtb_ref[...]
        dt = jnp.maximum(dtr, 0.0) + jnp.log1p(jnp.exp(-jnp.abs(dtr)))
        a = dt * (-jnp.exp(alog_ref[...]))
        a1, a2, a3 = _split3(a)
        tril = causal.astype(BF16)
        cum = _dot(tril, a1) + _dot(tril, a2) + _dot(tril, a3)
        cum2 = cum * LOG2E
        cumt_ref[...] = cum2.T
        for gg in range(N_GROUPS):
            sh = (LANES - HEADS_PER_GROUP * gg) % LANES
            cumg_ref[gg] = cum2 if sh == 0 else pltpu.roll(cum2, sh, axis=1)
        last = cum[CHUNK - 1:CHUNK, :]
        hq_ref[0] = dt.astype(BF16)
        hq_ref[1] = jnp.exp(cum).astype(BF16)
        hq_ref[2] = jnp.exp(last - cum).astype(BF16)
        dlast_ref[...] = jnp.broadcast_to(jnp.exp(last), (SUBLANES, LANES))

    @pl.when(c == 0)
    def _():
        tail_ref[...] = jnp.zeros(tail_ref.shape, F32)
        state_ref[...] = jnp.zeros(state_ref.shape, F32)

    chunk_quantities()

    causal_top = causal[0:HALF, 0:HALF]
    lane = lax.broadcasted_iota(jnp.int32, (CHUNK, LANES), 1)

    def group(g, carry):
        gx = pl.multiple_of(g * GROUP_W, GROUP_W)
        gb = pl.multiple_of(g * N_STATE, N_STATE)
        gc = pl.multiple_of(N_GROUPS * N_STATE + g * N_STATE, N_STATE)

        def conv_channels(ref):
            return jnp.concatenate([ref[:, pl.ds(gx, GROUP_W)],
                                    ref[:, pl.ds(pl.multiple_of(D_INNER + gb, N_STATE), N_STATE)],
                                    ref[:, pl.ds(pl.multiple_of(D_INNER + gc, N_STATE), N_STATE)]], axis=1)

        xin = jnp.concatenate([x_ref[:, pl.ds(gx, GROUP_W)], bc_ref[:, pl.ds(gb, N_STATE)],
                               bc_ref[:, pl.ds(gc, N_STATE)]], axis=1)
        xin_f = xin.astype(F32)
        cw = conv_channels(cw_ref)
        cbias = conv_channels(cbias_ref)
        halo_ref[0:SUBLANES, :] = tail_ref[g]
        halo_ref[SUBLANES:2 * SUBLANES, :] = jnp.zeros((SUBLANES, CONV_W), F32)
        tail_ref[g] = xin_f[CHUNK - SUBLANES:CHUNK, :]
        acc = cbias + cw[CONV_K - 1:CONV_K, :] * xin_f
        cor = jnp.zeros((SUBLANES, CONV_W), F32)
        for d in range(1, CONV_K):
            wk = cw[CONV_K - 1 - d:CONV_K - d, :]
            acc = acc + wk * _dot(shift_ref[d - 1], xin)
            cor = cor + wk * halo_ref[SUBLANES - d:2 * SUBLANES - d, :]
        acc = jnp.concatenate([acc[0:SUBLANES] + cor, acc[SUBLANES:]], axis=0)
        act = _silu(acc)
        xc = act[:, 0:GROUP_W]
        bmat = act[:, GROUP_W:GROUP_W + N_STATE].astype(BF16)
        cmat = act[:, GROUP_W + N_STATE:CONV_W].astype(BF16)

        eg = e_ref[g]
        e_dt = _dot(hq_ref[0], eg)
        e_cum = _dot(hq_ref[1], eg)
        e_end = _dot(hq_ref[2], eg)
        dh, dlo = _split2(dlast_ref[...])
        dl = _dot(dh, eg) + _dot(dlo, eg)

        cb = _dot_nt(cmat, bmat)
        cb_top = cb[0:HALF, 0:HALF]
        cb_bot = cb[HALF:CHUNK, :]
        xdt = xc * e_dt
        xdt_b = xdt.astype(BF16)
        xde_b = (xdt * e_end).astype(BF16)

        cg = cumg_ref[g]
        ys = []
        for pair in range(HEADS_PER_GROUP // 2):
            xp = xdt_b[:, LANES * pair:LANES * (pair + 1)]
            top = None
            bot = None
            for sub in range(2):
                hh = 2 * pair + sub
                colv = cg[:, hh:hh + 1]
                rowv = cumt_ref[pl.ds(g * HEADS_PER_GROUP + hh, 1), :]
                seg_t = jnp.where(causal_top, colv[0:HALF] - rowv[:, 0:HALF], NEG)
                seg_l = colv[HALF:CHUNK] - rowv[:, 0:HALF]
                seg_r = jnp.where(causal_top, colv[HALF:CHUNK] - rowv[:, HALF:CHUNK], NEG)
                w_t = (cb_top * jnp.exp2(seg_t)).astype(BF16)
                w_b = (cb_bot * jnp.exp2(jnp.concatenate([seg_l, seg_r], axis=1))).astype(BF16)
                in_head = (lane >= HEAD_DIM * sub) & (lane < HEAD_DIM * (sub + 1))
                rhs = jnp.where(in_head, xp, jnp.zeros_like(xp))
                t_t = _dot(w_t, rhs[0:HALF])
                t_b = _dot(w_b, rhs)
                top = t_t if top is None else top + t_t
                bot = t_b if bot is None else bot + t_b
            ys.append(jnp.concatenate([top, bot], axis=0))
        y = jnp.concatenate(ys, axis=1)

        st = state_ref[g]
        y = y + _dot(cmat, st.astype(BF16)) * e_cum
        y = y + dskip_ref[:, pl.ds(gx, GROUP_W)] * xc
        y = y * _silu(z_ref[:, pl.ds(gx, GROUP_W)].astype(F32))
        ms = jnp.mean(y * y, axis=-1, keepdims=True)
        y = y * lax.rsqrt(ms + EPS) * nw_ref[:, pl.ds(gx, GROUP_W)]
        y_ref[:, pl.ds(gx, GROUP_W)] = y.astype(y_ref.dtype)

        state_ref[g] = st * dl[0:1, :] + _dot_tn(bmat, xde_b)
        return carry

    def group_run(t, carry):
        for v in range(SSD_GROUPS_PER_TRIP):
            group(SSD_GROUPS_PER_TRIP * t + v, carry)
        return carry

    group_run(jnp.int32(0), 0)
    lax.fori_loop(1, N_GROUPS // SSD_GROUPS_PER_TRIP, group_run, 0)


def _ssd_branch(proj, dt_raw, conv_w, conv_b, dtb_pad, alog_pad, dskip_exp, ssm_norm_w, expand, shifts,
                batch, seq):
    t = proj.shape[0]
    nc = seq // CHUNK

    bc_w = 2 * N_GROUPS * N_STATE
    conv_ch = D_INNER + bc_w

    def rows(b, c):
        return b * nc + c

    in_specs = [
        pl.BlockSpec((CHUNK, D_INNER), lambda b, c: (rows(b, c), OFF_Z // D_INNER)),
        pl.BlockSpec((CHUNK, D_INNER), lambda b, c: (rows(b, c), OFF_X // D_INNER)),
        pl.BlockSpec((CHUNK, bc_w), lambda b, c: (rows(b, c), OFF_B // bc_w)),
        pl.BlockSpec((CHUNK, LANES), lambda b, c: (rows(b, c), 0)),
        pl.BlockSpec((CONV_K, conv_ch), lambda b, c: (0, 0)),
        pl.BlockSpec((1, conv_ch), lambda b, c: (0, 0)),
        pl.BlockSpec((1, LANES), lambda b, c: (0, 0)),
        pl.BlockSpec((1, LANES), lambda b, c: (0, 0)),
        pl.BlockSpec((1, D_INNER), lambda b, c: (0, 0)),
        pl.BlockSpec((1, D_INNER), lambda b, c: (0, 0)),
        pl.BlockSpec((N_GROUPS, LANES, GROUP_W), lambda b, c: (0, 0, 0)),
        pl.BlockSpec((CONV_K - 1, CHUNK, CHUNK), lambda b, c: (0, 0, 0)),
    ]
    return pl.pallas_call(
        _ssd_kernel,
        grid=(batch, nc),
        in_specs=in_specs,
        out_specs=pl.BlockSpec((CHUNK, D_INNER), lambda b, c: (rows(b, c), 0)),
        out_shape=jax.ShapeDtypeStruct((t, D_INNER), BF16),
        scratch_shapes=[
            pltpu.VMEM((N_GROUPS, N_STATE, GROUP_W), F32),
            pltpu.VMEM((N_GROUPS, SUBLANES, CONV_W), F32),
            pltpu.VMEM((2 * SUBLANES, CONV_W), F32),
            pltpu.VMEM((N_GROUPS, CHUNK, LANES), F32),
            pltpu.VMEM((LANES, CHUNK), F32),
            pltpu.VMEM((3, CHUNK, LANES), BF16),
            pltpu.VMEM((SUBLANES, LANES), F32),
        ],
        compiler_params=pltpu.CompilerParams(
            dimension_semantics=("arbitrary", "arbitrary"), vmem_limit_bytes=VMEM_LIMIT),
        name="ssd_branch",
    )(proj, proj, proj, dt_raw, conv_w, conv_b, dtb_pad, alog_pad, dskip_exp, ssm_norm_w, expand, shifts)


HEADS_PER_STEP = 4
STEP_W = HEADS_PER_STEP * ATTN_DH
KV_PER_ITER = 2
GROUPS_PER_TRIP = 8
ACC_ROWS = ATTN_DH + BF16_ROWS
EXP2_SCALE = (ATTN_DH ** -0.5) * LOG2E


def _moba_kernel(q_ref, k_ref, v_ref, g_ref, o_ref,
                 vt_ref, kmean_ref, bias_ref, sa_ref, mxa_ref, sb_ref, mxb_ref, acc_ref, m_ref):
    i = pl.program_id(2)
    seq = k_ref.shape[0]
    nb = seq // MOBA_BLOCK
    n_groups = nb // KV_PER_ITER

    @pl.when(i == 0)
    def _():
        def prep(jb, carry):
            off = pl.multiple_of(jb * MOBA_BLOCK, MOBA_BLOCK)
            vt = v_ref[pl.ds(off, MOBA_BLOCK), :].astype(F32).T.astype(BF16)
            for a in range(HEADS_PER_STEP):
                vt_ref[jb, a, 0:ATTN_DH, :] = vt[ATTN_DH * a:ATTN_DH * (a + 1), :]
                vt_ref[jb, a, ATTN_DH:ACC_ROWS, :] = jnp.ones((BF16_ROWS, MOBA_BLOCK), BF16)
            kb = k_ref[pl.ds(off, MOBA_BLOCK), :].astype(F32)
            kmean_ref[pl.ds(jb, 1), :] = jnp.mean(kb, axis=0, keepdims=True)
            return carry
        lax.fori_loop(0, nb, prep, 0)

    km_hi, km_lo = _split2(kmean_ref[...])
    blk = lax.broadcasted_iota(jnp.int32, (nb, MOBA_BLOCK), 0)
    past = blk < i
    key = lax.broadcasted_iota(jnp.int32, (MOBA_BLOCK, MOBA_BLOCK), 0)
    qry = lax.broadcasted_iota(jnp.int32, (MOBA_BLOCK, MOBA_BLOCK), 1)
    own = pl.multiple_of(i * MOBA_BLOCK, MOBA_BLOCK)

    def scores(off, a, qa):
        kk = k_ref[pl.ds(off, MOBA_BLOCK), ATTN_DH * a:ATTN_DH * (a + 1)]
        return _dot_nt(jnp.concatenate([kk, kk], axis=1), qa)

    heads = range(HEADS_PER_STEP)
    qs = [q_ref[:, ATTN_DH * a:ATTN_DH * (a + 1)] for a in heads]
    qas = []
    for a in heads:
        q_hi, q_lo = _split2(qs[a].astype(F32) * EXP2_SCALE)
        qas.append(jnp.concatenate([q_hi, q_lo], axis=1))

    gates = []
    for a in heads:
        lo, hi = ATTN_DH * a, ATTN_DH * (a + 1)
        gate = _dot_nt(km_hi[:, lo:hi], qs[a]) + _dot_nt(km_lo[:, lo:hi], qs[a])
        gates.append(jnp.where(past, gate, NEG))
    sels = [jnp.zeros((nb, MOBA_BLOCK), jnp.bool_) for a in heads]
    for _ in range(MOBA_TOPK):
        for a in heads:
            mx = jnp.max(gates[a], axis=0, keepdims=True)
            idx = jnp.min(jnp.where(gates[a] == mx, blk, 4 * nb), axis=0, keepdims=True)
            hit = blk == idx
            sels[a] = sels[a] | (hit & past)
            gates[a] = jnp.where(hit, -jnp.inf, gates[a])
    for a in heads:
        bias_ref[a] = jnp.where(sels[a], 0.0, NEG)

    for a in heads:
        s = jnp.where(key <= qry, scores(own, a, qas[a]), NEG)
        m = jnp.max(s, axis=0, keepdims=True)
        p = jnp.exp2(s - m)
        acc_ref[a] = _dot(vt_ref[i, a], p.astype(BF16))
        m_ref[a:a + 1, :] = m

    def score_tile(a, u, grp, s_ref, mx_ref):
        j = KV_PER_ITER * grp + u
        s = scores(pl.multiple_of(j * MOBA_BLOCK, MOBA_BLOCK), a, qas[a])
        s_ref[a, u] = s
        r = a * KV_PER_ITER + u
        mx_ref[r:r + 1, :] = jnp.max(s, axis=0, keepdims=True) + bias_ref[a, pl.ds(j, 1), :]

    def score_stage(a, grp, s_ref, mx_ref):
        for u in range(KV_PER_ITER):
            score_tile(a, u, grp, s_ref, mx_ref)

    def half_step(grp, cur, nxt):
        s_ref, mx_ref = cur
        nxt_grp = jnp.minimum(grp + 1, n_groups - 1)
        for a in range(HEADS_PER_STEP):
            m = m_ref[a:a + 1, :]
            mx = jnp.max(mx_ref[a * KV_PER_ITER:(a + 1) * KV_PER_ITER, :], axis=0, keepdims=True)
            m_new = jnp.maximum(m, mx)
            acc = jnp.exp2(m - m_new) * acc_ref[a]
            for u in range(KV_PER_ITER):
                j = KV_PER_ITER * grp + u
                m_eff = m_new - bias_ref[a, pl.ds(j, 1), :]
                p = jnp.exp2(s_ref[a, u] - m_eff)
                acc = acc + _dot(vt_ref[j, a], p.astype(BF16))
            acc_ref[a] = acc
            m_ref[a:a + 1, :] = m_new
            if nxt is not None:
                score_stage(a, nxt_grp, *nxt)

    buf_a = (sa_ref, mxa_ref)
    buf_b = (sb_ref, mxb_ref)

    def pair(grp):
        half_step(grp, buf_a, buf_b)
        half_step(grp + 1, buf_b, buf_a)

    def trip(tt, carry):
        for v in range(0, GROUPS_PER_TRIP, 2):
            pair(GROUPS_PER_TRIP * tt + v)
        return carry

    n_iter = (i + KV_PER_ITER - 1) // KV_PER_ITER
    for a in range(HEADS_PER_STEP):
        score_stage(a, 0, *buf_a)
    lax.fori_loop(0, n_iter // GROUPS_PER_TRIP, trip, 0)
    rest = n_iter % GROUPS_PER_TRIP

    w = GROUPS_PER_TRIP // 2
    while w >= 2:
        @pl.when((rest // w) % 2 == 1)
        def _(w=w):
            for v in range(0, w, 2):
                pair(n_iter - rest % (2 * w) + v)
        w //= 2

    @pl.when(rest % 2 == 1)
    def _():
        half_step(n_iter - 1, buf_a, None)

    for a in range(HEADS_PER_STEP):
        lo, hi = ATTN_DH * a, ATTN_DH * (a + 1)
        acc = acc_ref[a]
        denom = acc[ATTN_DH:ATTN_DH + 1, :]
        o = (acc[0:ATTN_DH, :] * (1.0 / denom)).T
        o_ref[:, lo:hi] = (o * _silu(g_ref[:, lo:hi].astype(F32))).astype(o_ref.dtype)


def _moba_branch(proj, batch, seq):
    t = proj.shape[0]
    nb = seq // MOBA_BLOCK
    assert nb % (2 * KV_PER_ITER) == 0

    def qrow(b, h, i):
        return b * nb + i

    score_buf = pltpu.VMEM((HEADS_PER_STEP, KV_PER_ITER, MOBA_BLOCK, MOBA_BLOCK), F32)
    max_buf = pltpu.VMEM((HEADS_PER_STEP * KV_PER_ITER, MOBA_BLOCK), F32)
    return pl.pallas_call(
        _moba_kernel,
        grid=(batch, ATTN_HEADS // HEADS_PER_STEP, nb),
        in_specs=[
            pl.BlockSpec((MOBA_BLOCK, STEP_W), lambda b, h, i: (qrow(b, h, i), OFF_Q // STEP_W + h)),
            pl.BlockSpec((seq, STEP_W), lambda b, h, i: (b, OFF_K // STEP_W + h)),
            pl.BlockSpec((seq, STEP_W), lambda b, h, i: (b, OFF_V // STEP_W + h)),
            pl.BlockSpec((MOBA_BLOCK, STEP_W), lambda b, h, i: (qrow(b, h, i), OFF_GA // STEP_W + h)),
        ],
        out_specs=pl.BlockSpec((MOBA_BLOCK, STEP_W), lambda b, h, i: (qrow(b, h, i), h)),
        out_shape=jax.ShapeDtypeStruct((t, ATTN_HEADS * ATTN_DH), BF16),
        scratch_shapes=[
            pltpu.VMEM((nb, HEADS_PER_STEP, ACC_ROWS, MOBA_BLOCK), BF16),
            pltpu.VMEM((nb, STEP_W), F32),
            pltpu.VMEM((HEADS_PER_STEP, nb, MOBA_BLOCK), F32),
            score_buf, max_buf, score_buf, max_buf,
            pltpu.VMEM((HEADS_PER_STEP, ACC_ROWS, MOBA_BLOCK), F32),
            pltpu.VMEM((SUBLANES, MOBA_BLOCK), F32),
        ],
        compiler_params=pltpu.CompilerParams(
            dimension_semantics=("arbitrary", "arbitrary", "arbitrary"), vmem_limit_bytes=VMEM_LIMIT),
        name="moba_branch",
    )(proj, proj, proj, proj)


MIX_TM = 1024
MIX_TN = 512
OUT_TM = 512


def _mix_kernel(yn_ref, og_ref, ws_ref, wa_ref, gs_ref, ga_ref, bs_ref, ba_ref, o_ref):
    y_ssm = _dot(yn_ref[...], ws_ref[...])
    y_att = _dot(og_ref[...], wa_ref[...])
    g_ssm = _sigmoid(gs_ref[...].astype(F32) + bs_ref[...])
    g_att = _sigmoid(ga_ref[...].astype(F32) + ba_ref[...])
    o_ref[...] = (g_ssm * y_ssm + g_att * y_att).astype(o_ref.dtype)


def _gated_merge(yn, og, w_ssm, w_att, proj, gate_bias):
    t = yn.shape[0]
    nj = D_MODEL // MIX_TN
    return pl.pallas_call(
        _mix_kernel,
        grid=(t // MIX_TM, nj),
        in_specs=[
            pl.BlockSpec((MIX_TM, D_INNER), lambda i, j: (i, 0)),
            pl.BlockSpec((MIX_TM, D_MODEL), lambda i, j: (i, 0)),
            pl.BlockSpec((D_INNER, MIX_TN), lambda i, j: (0, j)),
            pl.BlockSpec((D_MODEL, MIX_TN), lambda i, j: (0, j)),
            pl.BlockSpec((MIX_TM, MIX_TN), lambda i, j: (i, OFF_GM // MIX_TN + j)),
            pl.BlockSpec((MIX_TM, MIX_TN), lambda i, j: (i, OFF_GM // MIX_TN + nj + j)),
            pl.BlockSpec((1, MIX_TN), lambda i, j: (0, j)),
            pl.BlockSpec((1, MIX_TN), lambda i, j: (0, nj + j)),
        ],
        out_specs=pl.BlockSpec((MIX_TM, MIX_TN), lambda i, j: (i, j)),
        out_shape=jax.ShapeDtypeStruct((t, D_MODEL), BF16),
        compiler_params=pltpu.CompilerParams(
            dimension_semantics=("arbitrary", "arbitrary"), vmem_limit_bytes=VMEM_LIMIT),
        name="gated_merge",
    )(yn, og, w_ssm, w_att, proj, proj, gate_bias, gate_bias)


def _out_kernel(m_ref, w_ref, x_ref, fw_ref, o_ref):
    r = x_ref[...] + _dot(m_ref[...], w_ref[...])
    ms = jnp.mean(r * r, axis=-1, keepdims=True)
    o_ref[...] = r * lax.rsqrt(ms + EPS) * fw_ref[...]


def _output(mixed, w_out, x2, final_w):
    t = x2.shape[0]
    return pl.pallas_call(
        _out_kernel,
        grid=(t // OUT_TM,),
        in_specs=[
            pl.BlockSpec((OUT_TM, D_MODEL), lambda i: (i, 0)),
            pl.BlockSpec((D_MODEL, D_MODEL), lambda i: (0, 0)),
            pl.BlockSpec((OUT_TM, D_MODEL), lambda i: (i, 0)),
            pl.BlockSpec((1, D_MODEL), lambda i: (0, 0)),
        ],
        out_specs=pl.BlockSpec((OUT_TM, D_MODEL), lambda i: (i, 0)),
        out_shape=jax.ShapeDtypeStruct((t, D_MODEL), F32),
        compiler_params=pltpu.CompilerParams(
            dimension_semantics=("arbitrary",), vmem_limit_bytes=VMEM_LIMIT),
        name="output_norm",
    )(mixed, w_out, x2, final_w)


def _pad_lanes(v):
    return jnp.pad(v, (0, LANES - v.shape[0])).reshape(1, LANES)


def kernel(x, norm_w, w_in, conv_w, conv_b, dt_bias, A_log, D_skip, ssm_norm_w,
           w_ssm_proj, w_attn_proj, gate_bias, w_out, final_norm_w):
    batch, seq, _ = x.shape
    depth = norm_w.shape[0]
    assert depth == 1, "the block is built for a single layer"
    l = 0
    dt_lo = D_INNER + (D_INNER + 2 * N_GROUPS * N_STATE)
    dt_hi = dt_lo + N_HEADS

    head_of_channel = jnp.arange(D_INNER, dtype=jnp.int32) // HEAD_DIM
    expand = (jnp.arange(LANES, dtype=jnp.int32)[:, None] == head_of_channel[None, :]).astype(BF16)
    expand = expand.reshape(LANES, N_GROUPS, GROUP_W).transpose(1, 0, 2)
    rows = jnp.arange(CHUNK, dtype=jnp.int32)
    shifts = jnp.stack([(rows[:, None] - rows[None, :] == d).astype(BF16) for d in range(1, CONV_K)])

    x2 = x.reshape(batch * seq, D_MODEL)
    w = w_in[l]
    w_main = jnp.concatenate([w[:, :dt_lo], w[:, dt_hi:]], axis=1).astype(BF16)
    w_dt = jnp.pad(w[:, dt_lo:dt_hi], ((0, 0), (0, LANES - N_HEADS)))
    wdt_hi, wdt_lo = _split2(w_dt)

    proj, dt_raw = _in_projection(x2, norm_w[l].reshape(1, D_MODEL), w_main, wdt_hi, wdt_lo)

    yn = _ssd_branch(
        proj, dt_raw, conv_w[l], conv_b[l].reshape(1, -1),
        _pad_lanes(dt_bias[l]), _pad_lanes(A_log[l]),
        jnp.repeat(D_skip[l], HEAD_DIM).reshape(1, D_INNER),
        ssm_norm_w[l].reshape(1, D_INNER), expand, shifts, batch, seq)

    og = _moba_branch(proj, batch, seq)

    mixed = _gated_merge(yn, og, w_ssm_proj[l].astype(BF16), w_attn_proj[l].astype(BF16),
                         proj, gate_bias[l].reshape(1, -1))
    out = _output(mixed, w_out[l].astype(BF16), x2, final_norm_w.reshape(1, D_MODEL))
    return out.reshape(batch, seq, D_MODEL)
```
